```python
import jax, jax.numpy as jnp
from jax import lax
import numpy as np

D_MODEL = 1024
BATCH = 4
SEQ = 4096
DEPTH = 1

CHUNK = 64
Q_BLOCK = 128
HEAD_DIM = 64
N_SB_HEADS = 8
N_FOX_HEADS = 8
SB_WIDTH = N_SB_HEADS * HEAD_DIM
FOX_WIDTH = N_FOX_HEADS * HEAD_DIM
D_FF = -(-(8 * D_MODEL) // (3 * 256)) * 256
N_COND = 6
LN_EPS = 1e-5
DEEPNORM_ALPHA = (2 * DEPTH) ** 0.25
DEEPNORM_BETA = (8 * DEPTH) ** -0.25
OFF_SB = 0
OFF_FOX = OFF_SB + 3 * SB_WIDTH
OFF_FGATE = OFF_FOX + 3 * FOX_WIDTH
OFF_BGATE = OFF_FGATE + N_FOX_HEADS
IN_COLS = OFF_BGATE + 2 * D_MODEL

kernel_name = "hybrid_stickbreak_fox_gated_block"


def layer_norm(x, gain=None, bias=None):
    xf = x.astype(jnp.float32)
    mu = jnp.mean(xf, axis=-1, keepdims=True)
    var = jnp.mean(jnp.square(xf - mu), axis=-1, keepdims=True)
    y = (xf - mu) * lax.rsqrt(var + LN_EPS)
    if gain is not None:
        y = y * gain.astype(jnp.float32) + bias.astype(jnp.float32)
    return y.astype(x.dtype)


def modulate(x, shift, scale):
    return layer_norm(x) * (1.0 + scale[:, None, :]) + shift[:, None, :]


def split_heads(t, n_heads):
    b, s, _ = t.shape
    return t.reshape(b, s, n_heads, HEAD_DIM).transpose(0, 2, 1, 3)


def merge_heads(t):
    b, h, s, d = t.shape
    return t.transpose(0, 2, 1, 3).reshape(b, s, h * d)


def stick_breaking_attention(q, k, v):
    seq = q.shape[2]
    scale = HEAD_DIM ** -0.5
    outs = []
    for i in range(seq // Q_BLOCK):
        q0 = i * Q_BLOCK
        kv_len = q0 + Q_BLOCK
        qb = q[:, :, q0:kv_len]
        kb = k[:, :, :kv_len]
        vb = v[:, :, :kv_len]
        z = jnp.einsum('bhqd,bhkd->bhqk', qb, kb, preferred_element_type=jnp.float32) * scale
        qpos = q0 + jnp.arange(Q_BLOCK)[:, None]
        kpos = jnp.arange(kv_len)[None, :]
        mask = kpos < qpos
        log_1m = jnp.where(mask, jax.nn.log_sigmoid(-z), 0.0)
        suffix = lax.cumsum(log_1m, axis=3, reverse=True) - log_1m
        a = jnp.where(mask, jnp.exp(jax.nn.log_sigmoid(z) + suffix), 0.0)
        outs.append(jnp.einsum('bhqk,bhkd->bhqd', a.astype(v.dtype), vb))
    return jnp.concatenate(outs, axis=2)


def forgetting_attention(q, k, v, f_cum):
    seq = q.shape[2]
    scale = HEAD_DIM ** -0.5
    outs = []
    for i in range(seq // Q_BLOCK):
        q0 = i * Q_BLOCK
        kv_len = q0 + Q_BLOCK
        qb = q[:, :, q0:kv_len]
        kb = k[:, :, :kv_len]
        vb = v[:, :, :kv_len]
        z = (jnp.einsum('bhqd,bhkd->bhqk', qb, kb, preferred_element_type=jnp.float32) * scale
             + f_cum[:, :, q0:kv_len, None] - f_cum[:, :, None, :kv_len])
        qpos = q0 + jnp.arange(Q_BLOCK)[:, None]
        kpos = jnp.arange(kv_len)[None, :]
        p = jax.nn.softmax(jnp.where(kpos <= qpos, z, -jnp.inf), axis=-1)
        outs.append(jnp.einsum('bhqk,bhkd->bhqd', p.astype(v.dtype), vb))
    return jnp.concatenate(outs, axis=2)


def setup_inputs(seed: int = 0) -> dict:
    key = jax.random.key(seed)
    ks = jax.random.split(key, 20)
    f32 = jnp.float32
    nrm = lambda k, shape, s: (jax.random.normal(k, shape, f32) * s).astype(f32)
    return {
        "x": nrm(ks[0], (BATCH, SEQ, D_MODEL), 1.0),
        "c": nrm(ks[1], (BATCH, D_MODEL), 1.0),
        "w_ada": nrm(ks[2], (DEPTH, D_MODEL, N_COND * D_MODEL), 0.5 * D_MODEL ** -0.5),
        "b_ada": nrm(ks[3], (DEPTH, N_COND * D_MODEL), 0.02),
        "w_in": nrm(ks[4], (DEPTH, D_MODEL, IN_COLS), D_MODEL ** -0.5),
        "b_gate": nrm(ks[5], (DEPTH, 2 * D_MODEL), 0.02),
        "b_forget": 3.0 + nrm(ks[6], (DEPTH, N_FOX_HEADS), 1.0),
        "w_sb_out": nrm(ks[7], (DEPTH, SB_WIDTH, D_MODEL), SB_WIDTH ** -0.5),
        "w_fox_out": nrm(ks[8], (DEPTH, FOX_WIDTH, D_MODEL), FOX_WIDTH ** -0.5),
        "w_o": nrm(ks[9], (DEPTH, D_MODEL, D_MODEL), DEEPNORM_BETA * D_MODEL ** -0.5),
        "ln1_g": 1.0 + nrm(ks[10], (DEPTH, D_MODEL), 0.02),
        "ln1_b": nrm(ks[11], (DEPTH, D_MODEL), 0.02),
        "w_ffn_gate": nrm(ks[12], (DEPTH, D_MODEL, D_FF), D_MODEL ** -0.5),
        "w_ffn_up": nrm(ks[13], (DEPTH, D_MODEL, D_FF), D_MODEL ** -0.5),
        "w_ffn_down": nrm(ks[14], (DEPTH, D_FF, D_MODEL), DEEPNORM_BETA * D_FF ** -0.5),
        "ln2_g": 1.0 + nrm(ks[15], (DEPTH, D_MODEL), 0.02),
        "ln2_b": nrm(ks[16], (DEPTH, D_MODEL), 0.02),
    }


def reference(x, c, w_ada, b_ada, w_in, b_gate, b_forget, w_sb_out, w_fox_out, w_o,
              ln1_g, ln1_b, w_ffn_gate, w_ffn_up, w_ffn_down, ln2_g, ln2_b):
    c_act = jax.nn.silu(c)
    for l in range(DEPTH):
        ada = c_act @ w_ada[l] + b_ada[l]
        sh1, sc1, g1, sh2, sc2, g2 = jnp.split(ada, N_COND, axis=-1)

        u = modulate(x, sh1, sc1)
        proj = u @ w_in[l]
        q_sb = split_heads(proj[..., OFF_SB:OFF_SB + SB_WIDTH], N_SB_HEADS)
        k_sb = split_heads(proj[..., OFF_SB + SB_WIDTH:OFF_SB + 2 * SB_WIDTH], N_SB_HEADS)
        v_sb = split_heads(proj[..., OFF_SB + 2 * SB_WIDTH:OFF_FOX], N_SB_HEADS)
        q_fx = split_heads(proj[..., OFF_FOX:OFF_FOX + FOX_WIDTH], N_FOX_HEADS)
        k_fx = split_heads(proj[..., OFF_FOX + FOX_WIDTH:OFF_FOX + 2 * FOX_WIDTH], N_FOX_HEADS)
        v_fx = split_heads(proj[..., OFF_FOX + 2 * FOX_WIDTH:OFF_FGATE], N_FOX_HEADS)
        f_logit = proj[..., OFF_FGATE:OFF_BGATE].astype(jnp.float32) + b_forget[l].astype(jnp.float32)
        f_cum = jnp.cumsum(jax.nn.log_sigmoid(f_logit), axis=1).transpose(0, 2, 1)
        gate_logit = proj[..., OFF_BGATE:] + b_gate[l]
        g_sb = jax.nn.sigmoid(gate_logit[..., :D_MODEL])
        g_fx = jax.nn.sigmoid(gate_logit[..., D_MODEL:])

        y_sb = merge_heads(stick_breaking_attention(q_sb, k_sb, v_sb)) @ w_sb_out[l]
        y_fx = merge_heads(forgetting_attention(q_fx, k_fx, v_fx, f_cum)) @ w_fox_out[l]
        mix = (g_sb * y_sb + g_fx * y_fx) @ w_o[l]
        x = layer_norm(DEEPNORM_ALPHA * x + g1[:, None, :] * mix, ln1_g[l], ln1_b[l])

        u = modulate(x, sh2, sc2)
        h = (jax.nn.silu(u @ w_ffn_gate[l]) * (u @ w_ffn_up[l])) @ w_ffn_down[l]
        x = layer_norm(DEEPNORM_ALPHA * x + g2[:, None, :] * h, ln2_g[l], ln2_b[l])
    return x
```

```python
import functools

import jax
import jax.numpy as jnp
from jax import lax
from jax.experimental import pallas as pl
from jax.experimental.pallas import tpu as pltpu

HEAD_DIM = 64
N_HEADS = 8
BRANCH_WIDTH = N_HEADS * HEAD_DIM
LANES = 128
HEADS_PER_BLOCK = LANES // HEAD_DIM
LN_EPS = 1e-5
NEG_BIG = -1e30
VMEM_LIMIT = 56 * 1024 * 1024

ATTN_TILE = 256
ROW_TILE = 512
FFN_CHUNK = 256
ADA_COLS = 1024

f32 = jnp.float32
bf16 = jnp.bfloat16


def _dot(a, b):
    return jnp.dot(a, b, preferred_element_type=f32)


def _dot_nt(a, b):
    return lax.dot_general(a, b, (((1,), (1,)), ((), ())), preferred_element_type=f32)


def _layer_norm(x):
    mu = jnp.mean(x, axis=-1, keepdims=True)
    xc = x - mu
    var = jnp.mean(xc * xc, axis=-1, keepdims=True)
    return xc * lax.rsqrt(var + LN_EPS)


def _sigmoid(x):
    return 1.0 / (1.0 + jnp.exp(-x))


def _params(*sem):
    return pltpu.CompilerParams(dimension_semantics=sem, vmem_limit_bytes=VMEM_LIMIT)


def _ada_kernel(c_ref, w_ref, b_ref, o_ref):
    c = c_ref[...]
    c_act = (c * _sigmoid(c)).astype(bf16)
    o_ref[...] = _dot(c_act, w_ref[...].astype(bf16)) + b_ref[...]


def _ada(c_pad, w, b):
    rows, d = c_pad.shape
    n = w.shape[1]
    return pl.pallas_call(
        _ada_kernel,
        grid=(n // ADA_COLS,),
        in_specs=[
            pl.BlockSpec((rows, d), lambda j: (0, 0)),
            pl.BlockSpec((d, ADA_COLS), lambda j: (0, j)),
            pl.BlockSpec((1, ADA_COLS), lambda j: (0, j)),
        ],
        out_specs=pl.BlockSpec((rows, ADA_COLS), lambda j: (0, j)),
        out_shape=jax.ShapeDtypeStruct((rows, n), f32),
        compiler_params=_params("arbitrary"),
        name="ada",
    )(c_pad, w, b)


def _inproj_kernel(x_ref, sh_ref, sc_ref, wqkv_ref, wf_ref, qkv_ref, fl_ref, *, col_chunk):
    u = (_layer_norm(x_ref[...]) * (1.0 + sc_ref[0]) + sh_ref[0]).astype(bf16)
    n = wqkv_ref.shape[1]
    for c0 in range(0, n, col_chunk):
        qkv_ref[:, c0:c0 + col_chunk] = _dot(u, wqkv_ref[:, c0:c0 + col_chunk]).astype(bf16)
    fl_ref[...] = _dot(u, wf_ref[...])


def _inproj(x2, sh, sc, wqkv, wf, seq):
    n_tok, d = x2.shape
    tiles_per_seq = seq // ROW_TILE
    bmap = lambda r: (r // tiles_per_seq, 0, 0)
    return pl.pallas_call(
        functools.partial(_inproj_kernel, col_chunk=512),
        grid=(n_tok // ROW_TILE,),
        in_specs=[
            pl.BlockSpec((ROW_TILE, d), lambda r: (r, 0)),
            pl.BlockSpec((1, 1, d), bmap),
            pl.BlockSpec((1, 1, d), bmap),
            pl.BlockSpec(wqkv.shape, lambda r: (0, 0)),
            pl.BlockSpec(wf.shape, lambda r: (0, 0)),
        ],
        out_specs=[
            pl.BlockSpec((ROW_TILE, wqkv.shape[1]), lambda r: (r, 0)),
            pl.BlockSpec((ROW_TILE, LANES), lambda r: (r, 0)),
        ],
        out_shape=[
            jax.ShapeDtypeStruct((n_tok, wqkv.shape[1]), bf16),
            jax.ShapeDtypeStruct((n_tok, LANES), f32),
        ],
        compiler_params=_params("arbitrary"),
        name="inproj",
    )(x2, sh, sc, wqkv, wf)


def _split3(x):
    hi = x.astype(bf16)
    r1 = x - hi.astype(f32)
    mid = r1.astype(bf16)
    lo = (r1 - mid.astype(f32)).astype(bf16)
    return hi, mid, lo


def _fcum_kernel(fl_ref, bf_ref, tri_ref, o_ref, *, chunk):
    seq = fl_ref.shape[1]
    tri = tri_ref[...]
    carry = jnp.zeros((N_HEADS, 1), f32)
    for c0 in range(0, seq, chunk):
        logit = fl_ref[0, c0:c0 + chunk, :] + bf_ref[...]
        ls = jnp.minimum(logit, 0.0) - jnp.log1p(jnp.exp(-jnp.abs(logit)))
        ls_t = ls.T[0:N_HEADS, :]
        hi, mid, lo = _split3(ls_t)
        cs = _dot(hi, tri) + _dot(mid, tri) + _dot(lo, tri) + carry
        o_ref[0, :, c0:c0 + chunk] = cs
        carry = cs[:, chunk - 1:chunk]


def _fcum(fl3, bf_pad, tri):
    b, seq, _ = fl3.shape
    chunk = tri.shape[0]
    return pl.pallas_call(
        functools.partial(_fcum_kernel, chunk=chunk),
        grid=(b,),
        in_specs=[
            pl.BlockSpec((1, seq, LANES), lambda i: (i, 0, 0)),
            pl.BlockSpec((1, LANES), lambda i: (0, 0)),
            pl.BlockSpec(tri.shape, lambda i: (0, 0)),
        ],
        out_specs=pl.BlockSpec((1, N_HEADS, seq), lambda i: (i, 0, 0)),
        out_shape=jax.ShapeDtypeStruct((b, N_HEADS, seq), f32),
        compiler_params=_params("arbitrary"),
        name="fcum",
    )(fl3, bf_pad, tri)


def _head_queries(q):
    lane = lax.broadcasted_iota(jnp.int32, q.shape, 1)
    zero = jnp.zeros_like(q)
    return [jnp.where(lane < HEAD_DIM, q, zero), jnp.where(lane >= HEAD_DIM, q, zero)]


def _lane_tile(x, width):
    reps = width // LANES
    return x if reps == 1 else jnp.concatenate([x] * reps, axis=1)


def _merge_heads(o0, o1):
    lane = lax.broadcasted_iota(jnp.int32, o0.shape, 1)
    return jnp.where(lane < HEAD_DIM, o0, o1)


def _sb_kernel(q_ref, k_ref, v_ref, uu_ref, o_ref, acc_ref, car_ref, *, tile):
    i = pl.program_id(2)
    qh = _head_queries(q_ref[0])
    uu = uu_ref[...]
    row = lax.broadcasted_iota(jnp.int32, (tile, tile), 0)
    col = lax.broadcasted_iota(jnp.int32, (tile, tile), 1)
    below = col < row

    acc_ref[...] = jnp.zeros_like(acc_ref)
    car_ref[...] = jnp.zeros_like(car_ref)

    def step(j, diagonal):
        start = pl.multiple_of(j * tile, tile)
        k = k_ref[0, pl.ds(start, tile), :]
        v = v_ref[0, pl.ds(start, tile), :]
        for h in range(HEADS_PER_BLOCK):
            z = _dot_nt(qh[h], k)
            softplus = jnp.maximum(z, 0.0) + jnp.log(1.0 + jnp.exp(-jnp.abs(z)))
            log_1m = -softplus
            if diagonal:
                log_1m = jnp.where(below, log_1m, 0.0)
            hi = log_1m.astype(bf16)
            lo = (log_1m - hi.astype(f32)).astype(bf16)
            suffix = _dot(jnp.concatenate([hi, lo], axis=1), uu)
            g = suffix + _lane_tile(car_ref[h], tile)
            a = jnp.exp((z - softplus) + g)
            if diagonal:
                a = jnp.where(below, a, 0.0)
            acc_ref[h] += _dot(a.astype(bf16), v)
            total = jnp.sum(log_1m, axis=1, keepdims=True)
            car_ref[h] += jnp.broadcast_to(total, (tile, LANES))

    step(i, True)

    def body(jj, carry):
        step(i - 1 - jj, False)
        return carry

    lax.fori_loop(0, i, body, 0)
    o_ref[0] = _merge_heads(acc_ref[0], acc_ref[1]).astype(o_ref.dtype)


def _sb_attention(qkv, uu, tile):
    b, seq, _ = qkv.shape
    blocks = BRANCH_WIDTH // LANES
    return pl.pallas_call(
        functools.partial(_sb_kernel, tile=tile),
        grid=(b, blocks, seq // tile),
        in_specs=[
            pl.BlockSpec((1, tile, LANES), lambda bi, p, i: (bi, i, p)),
            pl.BlockSpec((1, seq, LANES), lambda bi, p, i: (bi, 0, blocks + p)),
            pl.BlockSpec((1, seq, LANES), lambda bi, p, i: (bi, 0, 2 * blocks + p)),
            pl.BlockSpec(uu.shape, lambda bi, p, i: (0, 0)),
        ],
        out_specs=pl.BlockSpec((1, tile, LANES), lambda bi, p, i: (bi, i, p)),
        out_shape=jax.ShapeDtypeStruct((b, seq, BRANCH_WIDTH), bf16),
        scratch_shapes=[
            pltpu.VMEM((HEADS_PER_BLOCK, tile, LANES), f32),
            pltpu.VMEM((HEADS_PER_BLOCK, tile, LANES), f32),
        ],
        compiler_params=_params("parallel", "parallel", "arbitrary"),
        name="sb_attn",
    )(qkv, qkv, qkv, uu)


def _fox_kernel(q_ref, k_ref, v_ref, fc_ref, o_ref, acc_ref, m_ref, l_ref, *, tile, n_blocks):
    p = pl.program_id(1)
    i = pl.program_id(2)
    qh = _head_queries(q_ref[0])
    row = lax.broadcasted_iota(jnp.int32, (tile, tile), 0)
    col = lax.broadcasted_iota(jnp.int32, (tile, tile), 1)
    causal = col <= row

    acc_ref[...] = jnp.zeros_like(acc_ref)
    l_ref[...] = jnp.zeros_like(l_ref)
    m_ref[...] = jnp.full_like(m_ref, NEG_BIG)

    def step(j, diagonal):
        start = pl.multiple_of(j * tile, tile)
        k = k_ref[0, pl.ds(start, tile), :]
        v = v_ref[0, pl.ds(start, tile), :]
        for h in range(HEADS_PER_BLOCK):
            head = p * HEADS_PER_BLOCK + h
            f_key = fc_ref[0, pl.ds(head * n_blocks + j, 1), :]
            s = _dot_nt(qh[h], k) - f_key
            if diagonal:
                s = jnp.where(causal, s, NEG_BIG)
            m_prev = m_ref[h]
            m_new = jnp.maximum(m_prev, jnp.max(s, axis=1, keepdims=True))
            alpha = jnp.exp(m_prev - m_new)
            pr = jnp.exp(s - _lane_tile(m_new, tile))
            l_ref[h] = alpha * l_ref[h] + jnp.sum(pr, axis=1, keepdims=True)
            acc_ref[h] = alpha * acc_ref[h] + _dot(pr.astype(bf16), v)
            m_ref[h] = m_new

    step(i, True)

    def body(jj, carry):
        step(i - 1 - jj, False)
        return carry

    lax.fori_loop(0, i, body, 0)
    o_ref[0] = _merge_heads(acc_ref[0] / l_ref[0], acc_ref[1] / l_ref[1]).astype(o_ref.dtype)


def _fox_attention(qkv, fc, tile):
    b, seq, _ = qkv.shape
    blocks = BRANCH_WIDTH // LANES
    n_blocks = seq // tile
    base = 3 * blocks
    return pl.pallas_call(
        functools.partial(_fox_kernel, tile=tile, n_blocks=n_blocks),
        grid=(b, blocks, n_blocks),
        in_specs=[
            pl.BlockSpec((1, tile, LANES), lambda bi, p, i: (bi, i, base + p)),
            pl.BlockSpec((1, seq, LANES), lambda bi, p, i: (bi, 0, base + blocks + p)),
            pl.BlockSpec((1, seq, LANES), lambda bi, p, i: (bi, 0, base + 2 * blocks + p)),
            pl.BlockSpec((1, N_HEADS * n_blocks, tile), lambda bi, p, i: (bi, 0, 0)),
        ],
        out_specs=pl.BlockSpec((1, tile, LANES), lambda bi, p, i: (bi, i, p)),
        out_shape=jax.ShapeDtypeStruct((b, seq, BRANCH_WIDTH), bf16),
        scratch_shapes=[
            pltpu.VMEM((HEADS_PER_BLOCK, tile, LANES), f32),
            pltpu.VMEM((HEADS_PER_BLOCK, tile, LANES), f32),
            pltpu.VMEM((HEADS_PER_BLOCK, tile, LANES), f32),
        ],
        compiler_params=_params("parallel", "parallel", "arbitrary"),
        name="fox_attn",
    )(qkv, qkv, qkv, fc)


def _post_kernel(x_ref, sh_ref, sc_ref, g1_ref, asb_ref, afx_ref, wg_ref, bg_ref, wsb_ref, wfx_ref,
                 wo_ref, lg_ref, lb_ref, o_ref, *, alpha):
    x = x_ref[...]
    d = x.shape[1]
    u = (_layer_norm(x) * (1.0 + sc_ref[0]) + sh_ref[0]).astype(bf16)
    y_sb = _dot(asb_ref[...], wsb_ref[...])
    y_fx = _dot(afx_ref[...], wfx_ref[...])
    g_sb = _sigmoid(_dot(u, wg_ref[:, 0:d]) + bg_ref[:, 0:d])
    g_fx = _sigmoid(_dot(u, wg_ref[:, d:2 * d]) + bg_ref[:, d:2 * d])
    mixed = (g_sb * y_sb + g_fx * y_fx).astype(bf16)
    mix = _dot(mixed, wo_ref[...])
    r = alpha * x + g1_ref[0] * mix
    o_ref[...] = _layer_norm(r) * lg_ref[...] + lb_ref[...]


def _post(x2, sh, sc, g1, asb, afx, wg, bg, wsb, wfx, wo, lg, lb, seq, alpha):
    n_tok, d = x2.shape
    tile = ROW_TILE // 2
    tiles_per_seq = seq // tile
    bmap = lambda r: (r // tiles_per_seq, 0, 0)
    full = lambda a: pl.BlockSpec(a.shape, lambda r: (0,) * a.ndim)
    return pl.pallas_call(
        functools.partial(_post_kernel, alpha=alpha),
        grid=(n_tok // tile,),
        in_specs=[
            pl.BlockSpec((tile, d), lambda r: (r, 0)),
            pl.BlockSpec((1, 1, d), bmap),
            pl.BlockSpec((1, 1, d), bmap),
            pl.BlockSpec((1, 1, d), bmap),
            pl.BlockSpec((tile, BRANCH_WIDTH), lambda r: (r, 0)),
            pl.BlockSpec((tile, BRANCH_WIDTH), lambda r: (r, 0)),
            full(wg), full(bg), full(wsb), full(wfx), full(wo), full(lg), full(lb),
        ],
        out_specs=pl.BlockSpec((tile, d), lambda r: (r, 0)),
        out_shape=jax.ShapeDtypeStruct((n_tok, d), f32),
        compiler_params=_params("arbitrary"),
        name="post",
    )(x2, sh, sc, g1, asb, afx, wg, bg, wsb, wfx, wo, lg, lb)


def _ffn_kernel(x_ref, sh_ref, sc_ref, g2_ref, wg_ref, wu_ref, wd_ref, lg_ref, lb_ref, o_ref, acc_ref,
                *, alpha):
    x = x_ref[...]
    u = (_layer_norm(x) * (1.0 + sc_ref[0]) + sh_ref[0]).astype(bf16)
    d_ff = wg_ref.shape[1]
    for c0 in range(0, d_ff, FFN_CHUNK):
        gate = _dot(u, wg_ref[:, c0:c0 + FFN_CHUNK])
        up = _dot(u, wu_ref[:, c0:c0 + FFN_CHUNK])
        h = (gate * _sigmoid(gate) * up).astype(bf16)
        part = _dot(h, wd_ref[c0:c0 + FFN_CHUNK, :])
        if c0 == 0:
            acc_ref[...] = part
        else:
            acc_ref[...] += part
    r = alpha * x + g2_ref[0] * acc_ref[...]
    o_ref[...] = _layer_norm(r) * lg_ref[...] + lb_ref[...]


def _ffn(x2, sh, sc, g2, wg, wu, wd, lg, lb, seq, alpha):
    n_tok, d = x2.shape
    tile = ROW_TILE
    tiles_per_seq = seq // tile
    bmap = lambda r: (r // tiles_per_seq, 0, 0)
    resident = lambda a: pl.BlockSpec(a.shape, lambda r: (0,) * a.ndim, pipeline_mode=pl.Buffered(1))
    return pl.pallas_call(
        functools.partial(_ffn_kernel, alpha=alpha),
        grid=(n_tok // tile,),
        in_specs=[
            pl.BlockSpec((tile, d), lambda r: (r, 0)),
            pl.BlockSpec((1, 1, d), bmap),
            pl.BlockSpec((1, 1, d), bmap),
            pl.BlockSpec((1, 1, d), bmap),
            resident(wg), resident(wu), resident(wd), resident(lg), resident(lb),
        ],
        out_specs=pl.BlockSpec((tile, d), lambda r: (r, 0)),
        out_shape=jax.ShapeDtypeStruct((n_tok, d), f32),
        scratch_shapes=[pltpu.VMEM((tile, d), f32)],
        compiler_params=_params("arbitrary"),
        name="ffn",
    )(x2, sh, sc, g2, wg, wu, wd, lg, lb)


def _strict_upper_pair(tile):
    r = jnp.arange(tile)[:, None]
    s = jnp.arange(tile)[None, :]
    u = (r > s).astype(bf16)
    return jnp.concatenate([u, u], axis=0)


def _inclusive_upper(chunk):
    r = jnp.arange(chunk)[:, None]
    s = jnp.arange(chunk)[None, :]
    return (r <= s).astype(bf16)


def kernel(x, c, w_ada, b_ada, w_in, b_gate, b_forget, w_sb_out, w_fox_out, w_o, ln1_g, ln1_b,
           w_ffn_gate, w_ffn_up, w_ffn_down, ln2_g, ln2_b):
    batch, seq, d = x.shape
    depth = w_ada.shape[0]
    alpha = (2 * depth) ** 0.25
    n_tok = batch * seq
    qkv_cols = 6 * BRANCH_WIDTH
    off_fgate = qkv_cols
    off_bgate = off_fgate + N_HEADS
    assert w_in.shape[2] == off_bgate + 2 * d
    assert seq % ATTN_TILE == 0 and seq % ROW_TILE == 0

    c_pad = jnp.zeros((8, d), f32).at[:batch].set(c)
    uu = _strict_upper_pair(ATTN_TILE)
    tri = _inclusive_upper(256)
    col_scale = jnp.ones((qkv_cols,), f32)
    col_scale = col_scale.at[0:BRANCH_WIDTH].set(HEAD_DIM ** -0.5)
    col_scale = col_scale.at[3 * BRANCH_WIDTH:4 * BRANCH_WIDTH].set(HEAD_DIM ** -0.5)

    x2 = x.reshape(n_tok, d)
    for l in range(depth):
        ada = _ada(c_pad, w_ada[l], b_ada[l][None, :])[:batch]
        sh1, sc1, g1, sh2, sc2, g2 = [t[:, None, :] for t in jnp.split(ada, 6, axis=-1)]

        wqkv = (w_in[l][:, :qkv_cols] * col_scale).astype(bf16)
        wf = jnp.zeros((d, LANES), f32).at[:, :N_HEADS].set(w_in[l][:, off_fgate:off_bgate]).astype(bf16)
        bf_pad = jnp.zeros((1, LANES), f32).at[0, :N_HEADS].set(b_forget[l])

        qkv, flog = _inproj(x2, sh1, sc1, wqkv, wf, seq)
        qkv = qkv.reshape(batch, seq, qkv_cols)
        fc = _fcum(flog.reshape(batch, seq, LANES), bf_pad, tri)
        fc = fc.reshape(batch, N_HEADS * (seq // ATTN_TILE), ATTN_TILE)

        a_sb = _sb_attention(qkv, uu, ATTN_TILE).reshape(n_tok, BRANCH_WIDTH)
        a_fx = _fox_attention(qkv, fc, ATTN_TILE).reshape(n_tok, BRANCH_WIDTH)

        x2 = _post(x2, sh1, sc1, g1, a_sb, a_fx,
                   w_in[l][:, off_bgate:].astype(bf16), b_gate[l][None, :],
                   w_sb_out[l].astype(bf16), w_fox_out[l].astype(bf16), w_o[l].astype(bf16),
                   ln1_g[l][None, :], ln1_b[l][None, :], seq, alpha)
        x2 = _ffn(x2, sh2, sc2, g2,
                  w_ffn_gate[l].astype(bf16), w_ffn_up[l].astype(bf16), w_ffn_down[l].astype(bf16),
                  ln2_g[l][None, :], ln2_b[l][None, :], seq, alpha)
    return x2.reshape(batch, seq, d)
```

```python
import functools

import jax
import jax.numpy as jnp
from jax import lax
from jax.experimental import pallas as pl
from jax.experimental.pallas import tpu as pltpu

HEAD_DIM = 64
N_HEADS = 8
BRANCH_WIDTH = N_HEADS * HEAD_DIM
LANES = 128
HEADS_PER_BLOCK = LANES // HEAD_DIM
LN_EPS = 1e-5
NEG_BIG = -1e30
VMEM_LIMIT = 56 * 1024 * 1024
SB_UNDERFLOW_LOG = -105.0

ATTN_TILE = 256
FOX_TILE = 512
ROW_TILE = 512
FFN_CHUNK = 256
ADA_COLS = 1024

f32 = jnp.float32
bf16 = jnp.bfloat16


def _dot(a, b):
    return jnp.dot(a, b, preferred_element_type=f32)


def _dot_nt(a, b):
    return lax.dot_general(a, b, (((1,), (1,)), ((), ())), preferred_element_type=f32)


def _layer_norm(x):
    mu = jnp.mean(x, axis=-1, keepdims=True)
    xc = x - mu
    var = jnp.mean(xc * xc, axis=-1, keepdims=True)
    return xc * lax.rsqrt(var + LN_EPS)


def _sigmoid(x):
    return 1.0 / (1.0 + jnp.exp(-x))


def _params(*sem):
    return pltpu.CompilerParams(dimension_semantics=sem, vmem_limit_bytes=VMEM_LIMIT)


def _ada_kernel(c_ref, w_ref, b_ref, o_ref):
    c = c_ref[...]
    c_act = (c * _sigmoid(c)).astype(bf16)
    o_ref[...] = _dot(c_act, w_ref[...].astype(bf16)) + b_ref[...]


def _ada(c_pad, w, b):
    rows, d = c_pad.shape
    n = w.shape[1]
    return pl.pallas_call(
        _ada_kernel,
        grid=(n // ADA_COLS,),
        in_specs=[
            pl.BlockSpec((rows, d), lambda j: (0, 0)),
            pl.BlockSpec((d, ADA_COLS), lambda j: (0, j)),
            pl.BlockSpec((1, ADA_COLS), lambda j: (0, j)),
        ],
        out_specs=pl.BlockSpec((rows, ADA_COLS), lambda j: (0, j)),
        out_shape=jax.ShapeDtypeStruct((rows, n), f32),
        compiler_params=_params("arbitrary"),
        name="ada",
    )(c_pad, w, b)


def _inproj_kernel(x_ref, sh_ref, sc_ref, wqkv_ref, wf_ref, qkv_ref, fl_ref, *, col_chunk):
    u = (_layer_norm(x_ref[...]) * (1.0 + sc_ref[0]) + sh_ref[0]).astype(bf16)
    n = wqkv_ref.shape[1]
    for c0 in range(0, n, col_chunk):
        qkv_ref[:, c0:c0 + col_chunk] = _dot(u, wqkv_ref[:, c0:c0 + col_chunk]).astype(bf16)
    fl_ref[...] = _dot(u, wf_ref[...])


def _inproj(x2, sh, sc, wqkv, wf, seq):
    n_tok, d = x2.shape
    tiles_per_seq = seq // ROW_TILE
    bmap = lambda r: (r // tiles_per_seq, 0, 0)
    return pl.pallas_call(
        functools.partial(_inproj_kernel, col_chunk=512),
        grid=(n_tok // ROW_TILE,),
        in_specs=[
            pl.BlockSpec((ROW_TILE, d), lambda r: (r, 0)),
            pl.BlockSpec((1, 1, d), bmap),
            pl.BlockSpec((1, 1, d), bmap),
            pl.BlockSpec(wqkv.shape, lambda r: (0, 0)),
            pl.BlockSpec(wf.shape, lambda r: (0, 0)),
        ],
        out_specs=[
            pl.BlockSpec((ROW_TILE, wqkv.shape[1]), lambda r: (r, 0)),
            pl.BlockSpec((ROW_TILE, LANES), lambda r: (r, 0)),
        ],
        out_shape=[
            jax.ShapeDtypeStruct((n_tok, wqkv.shape[1]), bf16),
            jax.ShapeDtypeStruct((n_tok, LANES), f32),
        ],
        compiler_params=_params("arbitrary"),
        name="inproj",
    )(x2, sh, sc, wqkv, wf)


def _split3(x):
    hi = x.astype(bf16)
    r1 = x - hi.astype(f32)
    mid = r1.astype(bf16)
    lo = (r1 - mid.astype(f32)).astype(bf16)
    return hi, mid, lo


def _fcum_kernel(fl_ref, bf_ref, tri_ref, o_ref, *, chunk):
    seq = fl_ref.shape[1]
    tri = tri_ref[...]
    carry = jnp.zeros((N_HEADS, 1), f32)
    for c0 in range(0, seq, chunk):
        logit = fl_ref[0, c0:c0 + chunk, :] + bf_ref[...]
        ls = jnp.minimum(logit, 0.0) - jnp.log1p(jnp.exp(-jnp.abs(logit)))
        ls_t = ls.T[0:N_HEADS, :]
        hi, mid, lo = _split3(ls_t)
        cs = _dot(hi, tri) + _dot(mid, tri) + _dot(lo, tri) + carry
        o_ref[0, :, c0:c0 + chunk] = cs
        carry = cs[:, chunk - 1:chunk]


def _fcum(fl3, bf_pad, tri):
    b, seq, _ = fl3.shape
    chunk = tri.shape[0]
    return pl.pallas_call(
        functools.partial(_fcum_kernel, chunk=chunk),
        grid=(b,),
        in_specs=[
            pl.BlockSpec((1, seq, LANES), lambda i: (i, 0, 0)),
            pl.BlockSpec((1, LANES), lambda i: (0, 0)),
            pl.BlockSpec(tri.shape, lambda i: (0, 0)),
        ],
        out_specs=pl.BlockSpec((1, N_HEADS, seq), lambda i: (i, 0, 0)),
        out_shape=jax.ShapeDtypeStruct((b, N_HEADS, seq), f32),
        compiler_params=_params("arbitrary"),
        name="fcum",
    )(fl3, bf_pad, tri)


def _head_queries(q):
    lane = lax.broadcasted_iota(jnp.int32, q.shape, 1)
    zero = jnp.zeros_like(q)
    return [jnp.where(lane < HEAD_DIM, q, zero), jnp.where(lane >= HEAD_DIM, q, zero)]


def _lane_tile(x, width):
    reps = width // LANES
    return x if reps == 1 else jnp.concatenate([x] * reps, axis=1)


def _merge_heads(o0, o1):
    lane = lax.broadcasted_iota(jnp.int32, o0.shape, 1)
    return jnp.where(lane < HEAD_DIM, o0, o1)


def _sb_kernel(q_ref, k_ref, v_ref, uu_ref, o_ref, acc_ref, car_ref, *, tile):
    i = pl.program_id(2)
    qh = _head_queries(q_ref[0])
    uu = uu_ref[...]
    row = lax.broadcasted_iota(jnp.int32, (tile, tile), 0)
    col = lax.broadcasted_iota(jnp.int32, (tile, tile), 1)
    below = col < row

    acc_ref[...] = jnp.zeros_like(acc_ref)
    car_ref[...] = jnp.zeros_like(car_ref)

    def step(j, diagonal):
        start = pl.multiple_of(j * tile, tile)
        k = k_ref[0, pl.ds(start, tile), :]
        v = v_ref[0, pl.ds(start, tile), :]
        for h in range(HEADS_PER_BLOCK):
            z = _dot_nt(qh[h], k)
            softplus = jnp.maximum(z, 0.0) + jnp.log(1.0 + jnp.exp(-jnp.abs(z)))
            log_1m = -softplus
            if diagonal:
                log_1m = jnp.where(below, log_1m, 0.0)
            hi = log_1m.astype(bf16)
            lo = (log_1m - hi.astype(f32)).astype(bf16)
            suffix = _dot(jnp.concatenate([hi, lo], axis=1), uu)
            g = suffix + _lane_tile(car_ref[h], tile)
            a = jnp.exp((z - softplus) + g)
            if diagonal:
                a = jnp.where(below, a, 0.0)
            acc_ref[h] += _dot(a.astype(bf16), v)
            total = jnp.sum(log_1m, axis=1, keepdims=True)
            car_ref[h] += jnp.broadcast_to(total, (tile, LANES))

    def exhausted():
        return jnp.max(jnp.maximum(car_ref[0], car_ref[1])) < SB_UNDERFLOW_LOG

    step(i, True)

    def cond(state):
        j, done = state
        return jnp.logical_and(j >= 0, jnp.logical_not(done))

    def body(state):
        j, _ = state
        step(j, False)
        return j - 1, exhausted()

    lax.while_loop(cond, body, (i - 1, exhausted()))
    o_ref[0] = _merge_heads(acc_ref[0], acc_ref[1]).astype(o_ref.dtype)


def _sb_attention(qkv, uu, tile):
    b, seq, _ = qkv.shape
    blocks = BRANCH_WIDTH // LANES
    return pl.pallas_call(
        functools.partial(_sb_kernel, tile=tile),
        grid=(b, blocks, seq // tile),
        in_specs=[
            pl.BlockSpec((1, tile, LANES), lambda bi, p, i: (bi, i, p)),
            pl.BlockSpec((1, seq, LANES), lambda bi, p, i: (bi, 0, blocks + p)),
            pl.BlockSpec((1, seq, LANES), lambda bi, p, i: (bi, 0, 2 * blocks + p)),
            pl.BlockSpec(uu.shape, lambda bi, p, i: (0, 0)),
        ],
        out_specs=pl.BlockSpec((1, tile, LANES), lambda bi, p, i: (bi, i, p)),
        out_shape=jax.ShapeDtypeStruct((b, seq, BRANCH_WIDTH), bf16),
        scratch_shapes=[
            pltpu.VMEM((HEADS_PER_BLOCK, tile, LANES), f32),
            pltpu.VMEM((HEADS_PER_BLOCK, tile, LANES), f32),
        ],
        compiler_params=_params("parallel", "parallel", "arbitrary"),
        name="sb_attn",
    )(qkv, qkv, qkv, uu)


def _lane_fold(x, op):
    return functools.reduce(op, [x[:, t * LANES:(t + 1) * LANES] for t in range(x.shape[1] // LANES)])


def _fox_kernel(q_ref, k_ref, v_ref, fc_ref, o_ref, s_ref, m_ref, l_ref, acc_ref, *, tile, n_blocks):
    p = pl.program_id(1)
    i = pl.program_id(2)
    qh = _head_queries(q_ref[0])
    row = lax.broadcasted_iota(jnp.int32, (tile, tile), 0)
    col = lax.broadcasted_iota(jnp.int32, (tile, tile), 1)
    causal = col <= row

    def scores(c, diagonal):
        start = pl.multiple_of(c * tile, tile)
        k = k_ref[0, pl.ds(start, tile), :]
        for h in range(HEADS_PER_BLOCK):
            head = p * HEADS_PER_BLOCK + h
            f_key = fc_ref[0, pl.ds(head * n_blocks + c, 1), :]
            s = _dot_nt(qh[h], k) - f_key
            if diagonal:
                s = jnp.where(causal, s, NEG_BIG)
                m_ref[h] = _lane_fold(s, jnp.maximum)
            else:
                m_ref[h] = jnp.maximum(m_ref[h], _lane_fold(s, jnp.maximum))
            s_ref[h, c] = s

    def weights(c, first):
        start = pl.multiple_of(c * tile, tile)
        v = v_ref[0, pl.ds(start, tile), :]
        for h in range(HEADS_PER_BLOCK):
            pr = jnp.exp(s_ref[h, c] - _lane_tile(m_ref[h], tile))
            pv = _dot(pr.astype(bf16), v)
            if first:
                l_ref[h] = _lane_fold(pr, jnp.add)
                acc_ref[h] = pv
            else:
                l_ref[h] += _lane_fold(pr, jnp.add)
                acc_ref[h] += pv

    def loop(fn):
        def body(c, carry):
            fn(c, False)
            return carry
        lax.fori_loop(0, i, body, 0)

    scores(i, True)
    loop(scores)
    for h in range(HEADS_PER_BLOCK):
        m_ref[h] = jnp.broadcast_to(jnp.max(m_ref[h], axis=1, keepdims=True), (tile, LANES))
    weights(i, True)
    loop(weights)
    out = []
    for h in range(HEADS_PER_BLOCK):
        denom = jnp.broadcast_to(jnp.sum(l_ref[h], axis=1, keepdims=True), (tile, LANES))
        out.append(acc_ref[h] / denom)
    o_ref[0] = _merge_heads(out[0], out[1]).astype(o_ref.dtype)


def _fox_attention(qkv, fc, tile):
    b, seq, _ = qkv.shape
    blocks = BRANCH_WIDTH // LANES
    n_blocks = seq // tile
    base = 3 * blocks
    return pl.pallas_call(
        functools.partial(_fox_kernel, tile=tile, n_blocks=n_blocks),
        grid=(b, blocks, n_blocks),
        in_specs=[
            pl.BlockSpec((1, tile, LANES), lambda bi, p, i: (bi, i, base + p)),
            pl.BlockSpec((1, seq, LANES), lambda bi, p, i: (bi, 0, base + blocks + p)),
            pl.BlockSpec((1, seq, LANES), lambda bi, p, i: (bi, 0, base + 2 * blocks + p)),
            pl.BlockSpec((1, N_HEADS * n_blocks, tile), lambda bi, p, i: (bi, 0, 0)),
        ],
        out_specs=pl.BlockSpec((1, tile, LANES), lambda bi, p, i: (bi, i, p)),
        out_shape=jax.ShapeDtypeStruct((b, seq, BRANCH_WIDTH), bf16),
        scratch_shapes=[
            pltpu.VMEM((HEADS_PER_BLOCK, n_blocks, tile, tile), f32),
            pltpu.VMEM((HEADS_PER_BLOCK, tile, LANES), f32),
            pltpu.VMEM((HEADS_PER_BLOCK, tile, LANES), f32),
            pltpu.VMEM((HEADS_PER_BLOCK, tile, LANES), f32),
        ],
        compiler_params=_params("parallel", "parallel", "arbitrary"),
        name="fox_attn",
    )(qkv, qkv, qkv, fc)


def _post_kernel(x_ref, sh_ref, sc_ref, g1_ref, asb_ref, afx_ref, wg_ref, bg_ref, wsb_ref, wfx_ref,
                 wo_ref, lg_ref, lb_ref, o_ref, *, alpha):
    x = x_ref[...]
    d = x.shape[1]
    u = (_layer_norm(x) * (1.0 + sc_ref[0]) + sh_ref[0]).astype(bf16)
    y_sb = _dot(asb_ref[...], wsb_ref[...])
    y_fx = _dot(afx_ref[...], wfx_ref[...])
    g_sb = _sigmoid(_dot(u, wg_ref[:, 0:d]) + bg_ref[:, 0:d])
    g_fx = _sigmoid(_dot(u, wg_ref[:, d:2 * d]) + bg_ref[:, d:2 * d])
    mixed = (g_sb * y_sb + g_fx * y_fx).astype(bf16)
    mix = _dot(mixed, wo_ref[...])
    r = alpha * x + g1_ref[0] * mix
    o_ref[...] = _layer_norm(r) * lg_ref[...] + lb_ref[...]


def _post(x2, sh, sc, g1, asb, afx, wg, bg, wsb, wfx, wo, lg, lb, seq, alpha):
    n_tok, d = x2.shape
    tile = ROW_TILE // 2
    tiles_per_seq = seq // tile
    bmap = lambda r: (r // tiles_per_seq, 0, 0)
    full = lambda a: pl.BlockSpec(a.shape, lambda r: (0,) * a.ndim)
    return pl.pallas_call(
        functools.partial(_post_kernel, alpha=alpha),
        grid=(n_tok // tile,),
        in_specs=[
            pl.BlockSpec((tile, d), lambda r: (r, 0)),
            pl.BlockSpec((1, 1, d), bmap),
            pl.BlockSpec((1, 1, d), bmap),
            pl.BlockSpec((1, 1, d), bmap),
            pl.BlockSpec((tile, BRANCH_WIDTH), lambda r: (r, 0)),
            pl.BlockSpec((tile, BRANCH_WIDTH), lambda r: (r, 0)),
            full(wg), full(bg), full(wsb), full(wfx), full(wo), full(lg), full(lb),
        ],
        out_specs=pl.BlockSpec((tile, d), lambda r: (r, 0)),
        out_shape=jax.ShapeDtypeStruct((n_tok, d), f32),
        compiler_params=_params("arbitrary"),
        name="post",
    )(x2, sh, sc, g1, asb, afx, wg, bg, wsb, wfx, wo, lg, lb)


def _ffn_kernel(x_ref, sh_ref, sc_ref, g2_ref, wg_ref, wu_ref, wd_ref, lg_ref, lb_ref, o_ref, acc_ref,
                *, alpha):
    x = x_ref[...]
    u = (_layer_norm(x) * (1.0 + sc_ref[0]) + sh_ref[0]).astype(bf16)
    d_ff = wg_ref.shape[1]
    for c0 in range(0, d_ff, FFN_CHUNK):
        gate = _dot(u, wg_ref[:, c0:c0 + FFN_CHUNK])
        up = _dot(u, wu_ref[:, c0:c0 + FFN_CHUNK])
        h = (gate * _sigmoid(gate) * up).astype(bf16)
        part = _dot(h, wd_ref[c0:c0 + FFN_CHUNK, :])
        if c0 == 0:
            acc_ref[...] = part
        else:
            acc_ref[...] += part
    r = alpha * x + g2_ref[0] * acc_ref[...]
    o_ref[...] = _layer_norm(r) * lg_ref[...] + lb_ref[...]


def _ffn(x2, sh, sc, g2, wg, wu, wd, lg, lb, seq, alpha):
    n_tok, d = x2.shape
    tile = ROW_TILE
    tiles_per_seq = seq // tile
    bmap = lambda r: (r // tiles_per_seq, 0, 0)
    resident = lambda a: pl.BlockSpec(a.shape, lambda r: (0,) * a.ndim, pipeline_mode=pl.Buffered(1))
    return pl.pallas_call(
        functools.partial(_ffn_kernel, alpha=alpha),
        grid=(n_tok // tile,),
        in_specs=[
            pl.BlockSpec((tile, d), lambda r: (r, 0)),
            pl.BlockSpec((1, 1, d), bmap),
            pl.BlockSpec((1, 1, d), bmap),
            pl.BlockSpec((1, 1, d), bmap),
            resident(wg), resident(wu), resident(wd), resident(lg), resident(lb),
        ],
        out_specs=pl.BlockSpec((tile, d), lambda r: (r, 0)),
        out_shape=jax.ShapeDtypeStruct((n_tok, d), f32),
        scratch_shapes=[pltpu.VMEM((tile, d), f32)],
        compiler_params=_params("arbitrary"),
        name="ffn",
    )(x2, sh, sc, g2, wg, wu, wd, lg, lb)


def _strict_upper_pair(tile):
    r = jnp.arange(tile)[:, None]
    s = jnp.arange(tile)[None, :]
    u = (r > s).astype(bf16)
    return jnp.concatenate([u, u], axis=0)


def _inclusive_upper(chunk):
    r = jnp.arange(chunk)[:, None]
    s = jnp.arange(chunk)[None, :]
    return (r <= s).astype(bf16)


def kernel(x, c, w_ada, b_ada, w_in, b_gate, b_forget, w_sb_out, w_fox_out, w_o, ln1_g, ln1_b,
           w_ffn_gate, w_ffn_up, w_ffn_down, ln2_g, ln2_b):
    batch, seq, d = x.shape
    depth = w_ada.shape[0]
    alpha = (2 * depth) ** 0.25
    n_tok = batch * seq
    qkv_cols = 6 * BRANCH_WIDTH
    off_fgate = qkv_cols
    off_bgate = off_fgate + N_HEADS
    assert w_in.shape[2] == off_bgate + 2 * d
    assert seq % ATTN_TILE == 0 and seq % FOX_TILE == 0 and seq % ROW_TILE == 0

    c_pad = jnp.zeros((8, d), f32).at[:batch].set(c)
    uu = _strict_upper_pair(ATTN_TILE)
    tri = _inclusive_upper(256)
    col_scale = jnp.ones((qkv_cols,), f32)
    col_scale = col_scale.at[0:BRANCH_WIDTH].set(HEAD_DIM ** -0.5)
    col_scale = col_scale.at[3 * BRANCH_WIDTH:4 * BRANCH_WIDTH].set(HEAD_DIM ** -0.5)

    x2 = x.reshape(n_tok, d)
    for l in range(depth):
        ada = _ada(c_pad, w_ada[l], b_ada[l][None, :])[:batch]
        sh1, sc1, g1, sh2, sc2, g2 = [t[:, None, :] for t in jnp.split(ada, 6, axis=-1)]

        wqkv = (w_in[l][:, :qkv_cols] * col_scale).astype(bf16)
        wf = jnp.zeros((d, LANES), f32).at[:, :N_HEADS].set(w_in[l][:, off_fgate:off_bgate]).astype(bf16)
        bf_pad = jnp.zeros((1, LANES), f32).at[0, :N_HEADS].set(b_forget[l])

        qkv, flog = _inproj(x2, sh1, sc1, wqkv, wf, seq)
        qkv = qkv.reshape(batch, seq, qkv_cols)
        fc = _fcum(flog.reshape(batch, seq, LANES), bf_pad, tri)
        fc = fc.reshape(batch, N_HEADS * (seq // FOX_TILE), FOX_TILE)

        a_sb = _sb_attention(qkv, uu, ATTN_TILE).reshape(n_tok, BRANCH_WIDTH)
        a_fx = _fox_attention(qkv, fc, FOX_TILE).reshape(n_tok, BRANCH_WIDTH)

        x2 = _post(x2, sh1, sc1, g1, a_sb, a_fx,
                   w_in[l][:, off_bgate:].astype(bf16), b_gate[l][None, :],
                   w_sb_out[l].astype(bf16), w_fox_out[l].astype(bf16), w_o[l].astype(bf16),
                   ln1_g[l][None, :], ln1_b[l][None, :], seq, alpha)
        x2 = _ffn(x2, sh2, sc2, g2,
                  w_ffn_gate[l].astype(bf16), w_ffn_up[l].astype(bf16), w_ffn_down[l].astype(bf16),
                  ln2_g[l][None, :], ln2_b[l][None, :], seq, alpha)
    return x2.reshape(batch, seq, d)
```

```python
import functools

import jax
import jax.numpy as jnp
from jax import lax
from jax.experimental import pallas as pl
from jax.experimental.pallas import tpu as pltpu

HEAD_DIM = 64
N_HEADS = 8
BRANCH_WIDTH = N_HEADS * HEAD_DIM
LANES = 128
HEADS_PER_BLOCK = LANES // HEAD_DIM
LN_EPS = 1e-5
NEG_BIG = -1e30
LOG2E = 1.4426950408889634
VMEM_LIMIT = 56 * 1024 * 1024
SB_UNDERFLOW_LOG = -105.0

ATTN_TILE = 256
FOX_TILE = 512
ROW_TILE = 512
FFN_CHUNK = 256
ADA_COLS = 1024

f32 = jnp.float32
bf16 = jnp.bfloat16


def _dot(a, b):
    return jnp.dot(a, b, preferred_element_type=f32)


def _dot_nt(a, b):
    return lax.dot_general(a, b, (((1,), (1,)), ((), ())), preferred_element_type=f32)


def _layer_norm(x):
    mu = jnp.mean(x, axis=-1, keepdims=True)
    xc = x - mu
    var = jnp.mean(xc * xc, axis=-1, keepdims=True)
    return xc * lax.rsqrt(var + LN_EPS)


def _sigmoid(x):
    return 1.0 / (1.0 + jnp.exp(-x))


def _params(*sem):
    return pltpu.CompilerParams(dimension_semantics=sem, vmem_limit_bytes=VMEM_LIMIT)


def _ada_kernel(c_ref, w_ref, b_ref, o_ref):
    c = c_ref[...]
    c_act = (c * _sigmoid(c)).astype(bf16)
    o_ref[...] = _dot(c_act, w_ref[...].astype(bf16)) + b_ref[...]


def _ada(c_pad, w, b):
    rows, d = c_pad.shape
    n = w.shape[1]
    return pl.pallas_call(
        _ada_kernel,
        grid=(n // ADA_COLS,),
        in_specs=[
            pl.BlockSpec((rows, d), lambda j: (0, 0)),
            pl.BlockSpec((d, ADA_COLS), lambda j: (0, j)),
            pl.BlockSpec((1, ADA_COLS), lambda j: (0, j)),
        ],
        out_specs=pl.BlockSpec((rows, ADA_COLS), lambda j: (0, j)),
        out_shape=jax.ShapeDtypeStruct((rows, n), f32),
        compiler_params=_params("arbitrary"),
        name="ada",
    )(c_pad, w, b)


def _inproj_kernel(x_ref, sh_ref, sc_ref, wqkv_ref, wf_ref, qkv_ref, fl_ref, *, col_chunk):
    u = (_layer_norm(x_ref[...]) * (1.0 + sc_ref[0]) + sh_ref[0]).astype(bf16)
    n = wqkv_ref.shape[1]
    for c0 in range(0, n, col_chunk):
        qkv_ref[:, c0:c0 + col_chunk] = _dot(u, wqkv_ref[:, c0:c0 + col_chunk]).astype(bf16)
    fl_ref[...] = _dot(u, wf_ref[...])


def _inproj(x2, sh, sc, wqkv, wf, seq):
    n_tok, d = x2.shape
    tiles_per_seq = seq // ROW_TILE
    bmap = lambda r: (r // tiles_per_seq, 0, 0)
    return pl.pallas_call(
        functools.partial(_inproj_kernel, col_chunk=512),
        grid=(n_tok // ROW_TILE,),
        in_specs=[
            pl.BlockSpec((ROW_TILE, d), lambda r: (r, 0)),
            pl.BlockSpec((1, 1, d), bmap),
            pl.BlockSpec((1, 1, d), bmap),
            pl.BlockSpec(wqkv.shape, lambda r: (0, 0)),
            pl.BlockSpec(wf.shape, lambda r: (0, 0)),
        ],
        out_specs=[
            pl.BlockSpec((ROW_TILE, wqkv.shape[1]), lambda r: (r, 0)),
            pl.BlockSpec((ROW_TILE, LANES), lambda r: (r, 0)),
        ],
        out_shape=[
            jax.ShapeDtypeStruct((n_tok, wqkv.shape[1]), bf16),
            jax.ShapeDtypeStruct((n_tok, LANES), f32),
        ],
        compiler_params=_params("arbitrary"),
        name="inproj",
    )(x2, sh, sc, wqkv, wf)


def _split3(x):
    hi = x.astype(bf16)
    r1 = x - hi.astype(f32)
    mid = r1.astype(bf16)
    lo = (r1 - mid.astype(f32)).astype(bf16)
    return hi, mid, lo


def _fcum_kernel(fl_ref, bf_ref, tri_ref, o_ref, *, chunk):
    seq = fl_ref.shape[1]
    tri = tri_ref[...]
    carry = jnp.zeros((N_HEADS, 1), f32)
    for c0 in range(0, seq, chunk):
        logit = fl_ref[0, c0:c0 + chunk, :] + bf_ref[...]
        ls = jnp.minimum(logit, 0.0) - jnp.log1p(jnp.exp(-jnp.abs(logit)))
        ls_t = ls.T[0:N_HEADS, :]
        hi, mid, lo = _split3(ls_t)
        cs = _dot(hi, tri) + _dot(mid, tri) + _dot(lo, tri) + carry
        o_ref[0, :, c0:c0 + chunk] = cs
        carry = cs[:, chunk - 1:chunk]


def _fcum(fl3, bf_pad, tri):
    b, seq, _ = fl3.shape
    chunk = tri.shape[0]
    return pl.pallas_call(
        functools.partial(_fcum_kernel, chunk=chunk),
        grid=(b,),
        in_specs=[
            pl.BlockSpec((1, seq, LANES), lambda i: (i, 0, 0)),
            pl.BlockSpec((1, LANES), lambda i: (0, 0)),
            pl.BlockSpec(tri.shape, lambda i: (0, 0)),
        ],
        out_specs=pl.BlockSpec((1, N_HEADS, seq), lambda i: (i, 0, 0)),
        out_shape=jax.ShapeDtypeStruct((b, N_HEADS, seq), f32),
        compiler_params=_params("arbitrary"),
        name="fcum",
    )(fl3, bf_pad, tri)


def _head_queries(q):
    lane = lax.broadcasted_iota(jnp.int32, q.shape, 1)
    zero = jnp.zeros_like(q)
    return [jnp.where(lane < HEAD_DIM, q, zero), jnp.where(lane >= HEAD_DIM, q, zero)]


def _lane_tile(x, width):
    reps = width // LANES
    return x if reps == 1 else jnp.concatenate([x] * reps, axis=1)


def _merge_heads(o0, o1):
    lane = lax.broadcasted_iota(jnp.int32, o0.shape, 1)
    return jnp.where(lane < HEAD_DIM, o0, o1)


def _neg_abs(x):
    bits = lax.bitcast_convert_type(x, jnp.uint32) | jnp.uint32(0x80000000)
    return lax.bitcast_convert_type(bits, f32)


def _sb_suffix(u, mask, uu):
    log_1m = jnp.minimum(u, 0.0) - jnp.log(1.0 + jnp.exp(_neg_abs(u)))
    if mask is not None:
        log_1m = jnp.where(mask, log_1m, 0.0)
    hi = log_1m.astype(bf16)
    lo = (log_1m - hi.astype(f32)).astype(bf16)
    return _dot(jnp.concatenate([hi, lo], axis=1), uu)


def _sb_total(incl):
    return jnp.broadcast_to(incl[:, 0:1], (incl.shape[0], LANES))


def _sb_weights(u, incl, carry, mask):
    g = incl if carry is None else incl + _lane_tile(carry, u.shape[1])
    a = jnp.exp(g - u)
    if mask is not None:
        a = jnp.where(mask, a, 0.0)
    return a.astype(bf16)


def _sb_block(u, carry, mask, uu):
    incl = _sb_suffix(u, mask, uu)
    return _sb_weights(u, incl, carry, mask), _sb_total(incl)


def _sb_kernel(q_ref, k_ref, v_ref, uu_ref, o_ref, acc_ref, car_ref, *, tile):
    i = pl.program_id(2)
    qh = _head_queries(-q_ref[0])
    uu = uu_ref[...]
    row = lax.broadcasted_iota(jnp.int32, (tile, tile), 0)
    col = lax.broadcasted_iota(jnp.int32, (tile, tile), 1)
    below = col < row

    @pl.when(i == 0)
    def _():
        k = k_ref[0, 0:tile, :]
        v = v_ref[0, 0:tile, :]
        for h in range(HEADS_PER_BLOCK):
            a, total = _sb_block(_dot_nt(qh[h], k), None, below, uu)
            acc_ref[h] = _dot(a, v)
            car_ref[h] = total

    @pl.when(i > 0)
    def _():
        start = pl.multiple_of((i - 1) * tile, tile)
        k = k_ref[0, pl.ds(start, 2 * tile), :]
        v = v_ref[0, pl.ds(start, 2 * tile), :]
        heads = range(HEADS_PER_BLOCK)
        u = [_dot_nt(qh[h], k) for h in heads]
        s_diag = [_sb_suffix(u[h][:, tile:], below, uu) for h in heads]
        s_prev = [_sb_suffix(u[h][:, :tile], None, uu) for h in heads]
        for h in heads:
            t_diag = _sb_total(s_diag[h])
            a_diag = _sb_weights(u[h][:, tile:], s_diag[h], None, below)
            a_prev = _sb_weights(u[h][:, :tile], s_prev[h], t_diag, None)
            acc_ref[h] = _dot(jnp.concatenate([a_prev, a_diag], axis=1), v)
            car_ref[h] = t_diag + _sb_total(s_prev[h])

    def exhausted():
        return jnp.max(jnp.maximum(car_ref[0], car_ref[1])) < SB_UNDERFLOW_LOG

    def cond(state):
        j, done = state
        return jnp.logical_and(j >= 0, jnp.logical_not(done))

    def body(state):
        j, _ = state
        start = pl.multiple_of(j * tile, tile)
        k = k_ref[0, pl.ds(start, tile), :]
        v = v_ref[0, pl.ds(start, tile), :]
        for h in range(HEADS_PER_BLOCK):
            a, total = _sb_block(_dot_nt(qh[h], k), car_ref[h], None, uu)
            acc_ref[h] += _dot(a, v)
            car_ref[h] += total
        return j - 1, exhausted()

    lax.while_loop(cond, body, (i - 2, exhausted()))
    o_ref[0] = _merge_heads(acc_ref[0], acc_ref[1]).astype(o_ref.dtype)


def _sb_attention(qkv, uu, tile):
    b, seq, _ = qkv.shape
    blocks = BRANCH_WIDTH // LANES
    return pl.pallas_call(
        functools.partial(_sb_kernel, tile=tile),
        grid=(b, blocks, seq // tile),
        in_specs=[
            pl.BlockSpec((1, tile, LANES), lambda bi, p, i: (bi, i, p)),
            pl.BlockSpec((1, seq, LANES), lambda bi, p, i: (bi, 0, blocks + p)),
            pl.BlockSpec((1, seq, LANES), lambda bi, p, i: (bi, 0, 2 * blocks + p)),
            pl.BlockSpec(uu.shape, lambda bi, p, i: (0, 0)),
        ],
        out_specs=pl.BlockSpec((1, tile, LANES), lambda bi, p, i: (bi, i, p)),
        out_shape=jax.ShapeDtypeStruct((b, seq, BRANCH_WIDTH), bf16),
        scratch_shapes=[
            pltpu.VMEM((HEADS_PER_BLOCK, tile, LANES), f32),
            pltpu.VMEM((HEADS_PER_BLOCK, tile, LANES), f32),
        ],
        compiler_params=_params("parallel", "parallel", "arbitrary"),
        name="sb_attn",
    )(qkv, qkv, qkv, uu)


def _lane_fold(x, op):
    return functools.reduce(op, [x[:, t * LANES:(t + 1) * LANES] for t in range(x.shape[1] // LANES)])


def _fox_kernel(q_ref, k_ref, v_ref, fc_ref, o_ref, s_ref, m_ref, acc_ref, *, tile, n_blocks):
    p = pl.program_id(1)
    i = pl.program_id(2)
    qh = _head_queries(q_ref[0])
    row = lax.broadcasted_iota(jnp.int32, (tile, tile), 0)
    col = lax.broadcasted_iota(jnp.int32, (tile, tile), 1)
    causal = col <= row

    def scores(c, diagonal):
        start = pl.multiple_of(c * tile, tile)
        k = k_ref[0, pl.ds(start, tile), :]
        for h in range(HEADS_PER_BLOCK):
            head = p * HEADS_PER_BLOCK + h
            f_key = fc_ref[0, pl.ds(head * n_blocks + c, 1), :]
            s = _dot_nt(qh[h], k) * LOG2E - f_key * LOG2E
            if diagonal:
                s = jnp.where(causal, s, NEG_BIG)
                m_ref[h] = _lane_fold(s, jnp.maximum)
            else:
                m_ref[h] = jnp.maximum(m_ref[h], _lane_fold(s, jnp.maximum))
            s_ref[h, c] = s

    def weights(c, first):
        start = pl.multiple_of(c * tile, tile)
        v = jnp.concatenate([v_ref[0, pl.ds(start, tile), :], jnp.ones((tile, LANES), bf16)], axis=1)
        for h in range(HEADS_PER_BLOCK):
            pr = jnp.exp2(s_ref[h, c] - _lane_tile(m_ref[h], tile))
            pv = _dot(pr.astype(bf16), v)
            if first:
                acc_ref[h] = pv
            else:
                acc_ref[h] += pv

    def loop(fn):
        def body(c, carry):
            fn(c, False)
            return carry
        lax.fori_loop(0, i, body, 0)

    scores(i, True)
    loop(scores)
    for h in range(HEADS_PER_BLOCK):
        m_ref[h] = jnp.broadcast_to(jnp.max(m_ref[h], axis=1, keepdims=True), (tile, LANES))
    weights(i, True)
    loop(weights)
    out = [acc_ref[h, :, 0:LANES] / acc_ref[h, :, LANES:2 * LANES] for h in range(HEADS_PER_BLOCK)]
    o_ref[0] = _merge_heads(out[0], out[1]).astype(o_ref.dtype)


def _fox_attention(qkv, fc, tile):
    b, seq, _ = qkv.shape
    blocks = BRANCH_WIDTH // LANES
    n_blocks = seq // tile
    base = 3 * blocks
    return pl.pallas_call(
        functools.partial(_fox_kernel, tile=tile, n_blocks=n_blocks),
        grid=(b, blocks, n_blocks),
        in_specs=[
            pl.BlockSpec((1, tile, LANES), lambda bi, p, i: (bi, i, base + p)),
            pl.BlockSpec((1, seq, LANES), lambda bi, p, i: (bi, 0, base + blocks + p)),
            pl.BlockSpec((1, seq, LANES), lambda bi, p, i: (bi, 0, base + 2 * blocks + p)),
            pl.BlockSpec((1, N_HEADS * n_blocks, tile), lambda bi, p, i: (bi, 0, 0)),
        ],
        out_specs=pl.BlockSpec((1, tile, LANES), lambda bi, p, i: (bi, i, p)),
        out_shape=jax.ShapeDtypeStruct((b, seq, BRANCH_WIDTH), bf16),
        scratch_shapes=[
            pltpu.VMEM((HEADS_PER_BLOCK, n_blocks, tile, tile), f32),
            pltpu.VMEM((HEADS_PER_BLOCK, tile, LANES), f32),
            pltpu.VMEM((HEADS_PER_BLOCK, tile, 2 * LANES), f32),
        ],
        compiler_params=_params("parallel", "parallel", "arbitrary"),
        name="fox_attn",
    )(qkv, qkv, qkv, fc)


def _post_kernel(x_ref, sh_ref, sc_ref, g1_ref, asb_ref, afx_ref, wg_ref, bg_ref, wsb_ref, wfx_ref,
                 wo_ref, lg_ref, lb_ref, o_ref, *, alpha):
    x = x_ref[...]
    d = x.shape[1]
    u = (_layer_norm(x) * (1.0 + sc_ref[0]) + sh_ref[0]).astype(bf16)
    y_sb = _dot(asb_ref[...], wsb_ref[...])
    y_fx = _dot(afx_ref[...], wfx_ref[...])
    g_sb = _sigmoid(_dot(u, wg_ref[:, 0:d]) + bg_ref[:, 0:d])
    g_fx = _sigmoid(_dot(u, wg_ref[:, d:2 * d]) + bg_ref[:, d:2 * d])
    mixed = (g_sb * y_sb + g_fx * y_fx).astype(bf16)
    mix = _dot(mixed, wo_ref[...])
    r = alpha * x + g1_ref[0] * mix
    o_ref[...] = _layer_norm(r) * lg_ref[...] + lb_ref[...]


def _post(x2, sh, sc, g1, asb, afx, wg, bg, wsb, wfx, wo, lg, lb, seq, alpha):
    n_tok, d = x2.shape
    tile = ROW_TILE // 2
    tiles_per_seq = seq // tile
    bmap = lambda r: (r // tiles_per_seq, 0, 0)
    full = lambda a: pl.BlockSpec(a.shape, lambda r: (0,) * a.ndim)
    return pl.pallas_call(
        functools.partial(_post_kernel, alpha=alpha),
        grid=(n_tok // tile,),
        in_specs=[
            pl.BlockSpec((tile, d), lambda r: (r, 0)),
            pl.BlockSpec((1, 1, d), bmap),
            pl.BlockSpec((1, 1, d), bmap),
            pl.BlockSpec((1, 1, d), bmap),
            pl.BlockSpec((tile, BRANCH_WIDTH), lambda r: (r, 0)),
            pl.BlockSpec((tile, BRANCH_WIDTH), lambda r: (r, 0)),
            full(wg), full(bg), full(wsb), full(wfx), full(wo), full(lg), full(lb),
        ],
        out_specs=pl.BlockSpec((tile, d), lambda r: (r, 0)),
        out_shape=jax.ShapeDtypeStruct((n_tok, d), f32),
        compiler_params=_params("arbitrary"),
        name="post",
    )(x2, sh, sc, g1, asb, afx, wg, bg, wsb, wfx, wo, lg, lb)


def _ffn_kernel(x_ref, sh_ref, sc_ref, g2_ref, wg_ref, wu_ref, wd_ref, lg_ref, lb_ref, o_ref, acc_ref,
                *, alpha):
    x = x_ref[...]
    u = (_layer_norm(x) * (1.0 + sc_ref[0]) + sh_ref[0]).astype(bf16)
    d_ff = wg_ref.shape[1]
    for c0 in range(0, d_ff, FFN_CHUNK):
        gate = _dot(u, wg_ref[:, c0:c0 + FFN_CHUNK])
        up = _dot(u, wu_ref[:, c0:c0 + FFN_CHUNK])
        h = (gate * _sigmoid(gate) * up).astype(bf16)
        part = _dot(h, wd_ref[c0:c0 + FFN_CHUNK, :])
        if c0 == 0:
            acc_ref[...] = part
        else:
            acc_ref[...] += part
    r = alpha * x + g2_ref[0] * acc_ref[...]
    o_ref[...] = _layer_norm(r) * lg_ref[...] + lb_ref[...]


def _ffn(x2, sh, sc, g2, wg, wu, wd, lg, lb, seq, alpha):
    n_tok, d = x2.shape
    tile = ROW_TILE
    tiles_per_seq = seq // tile
    bmap = lambda r: (r // tiles_per_seq, 0, 0)
    resident = lambda a: pl.BlockSpec(a.shape, lambda r: (0,) * a.ndim, pipeline_mode=pl.Buffered(1))
    return pl.pallas_call(
        functools.partial(_ffn_kernel, alpha=alpha),
        grid=(n_tok // tile,),
        in_specs=[
            pl.BlockSpec((tile, d), lambda r: (r, 0)),
            pl.BlockSpec((1, 1, d), bmap),
            pl.BlockSpec((1, 1, d), bmap),
            pl.BlockSpec((1, 1, d), bmap),
            resident(wg), resident(wu), resident(wd), resident(lg), resident(lb),
        ],
        out_specs=pl.BlockSpec((tile, d), lambda r: (r, 0)),
        out_shape=jax.ShapeDtypeStruct((n_tok, d), f32),
        scratch_shapes=[pltpu.VMEM((tile, d), f32)],
        compiler_params=_params("arbitrary"),
        name="ffn",
    )(x2, sh, sc, g2, wg, wu, wd, lg, lb)


def _suffix_sum_pair(tile):
    r = jnp.arange(tile)[:, None]
    s = jnp.arange(tile)[None, :]
    u = (r >= s).astype(bf16)
    return jnp.concatenate([u, u], axis=0)


def _inclusive_upper(chunk):
    r = jnp.arange(chunk)[:, None]
    s = jnp.arange(chunk)[None, :]
    return (r <= s).astype(bf16)


def kernel(x, c, w_ada, b_ada, w_in, b_gate, b_forget, w_sb_out, w_fox_out, w_o, ln1_g, ln1_b,
           w_ffn_gate, w_ffn_up, w_ffn_down, ln2_g, ln2_b):
    batch, seq, d = x.shape
    depth = w_ada.shape[0]
    alpha = (2 * depth) ** 0.25
    n_tok = batch * seq
    qkv_cols = 6 * BRANCH_WIDTH
    off_fgate = qkv_cols
    off_bgate = off_fgate + N_HEADS
    assert w_in.shape[2] == off_bgate + 2 * d
    assert seq % ATTN_TILE == 0 and seq % FOX_TILE == 0 and seq % ROW_TILE == 0

    c_pad = jnp.zeros((8, d), f32).at[:batch].set(c)
    uu = _suffix_sum_pair(ATTN_TILE)
    tri = _inclusive_upper(256)
    col_scale = jnp.ones((qkv_cols,), f32)
    col_scale = col_scale.at[0:BRANCH_WIDTH].set(HEAD_DIM ** -0.5)
    col_scale = col_scale.at[3 * BRANCH_WIDTH:4 * BRANCH_WIDTH].set(HEAD_DIM ** -0.5)

    x2 = x.reshape(n_tok, d)
    for l in range(depth):
        ada = _ada(c_pad, w_ada[l], b_ada[l][None, :])[:batch]
        sh1, sc1, g1, sh2, sc2, g2 = [t[:, None, :] for t in jnp.split(ada, 6, axis=-1)]

        wqkv = (w_in[l][:, :qkv_cols] * col_scale).astype(bf16)
        wf = jnp.zeros((d, LANES), f32).at[:, :N_HEADS].set(w_in[l][:, off_fgate:off_bgate]).astype(bf16)
        bf_pad = jnp.zeros((1, LANES), f32).at[0, :N_HEADS].set(b_forget[l])

        qkv, flog = _inproj(x2, sh1, sc1, wqkv, wf, seq)
        qkv = qkv.reshape(batch, seq, qkv_cols)
        fc = _fcum(flog.reshape(batch, seq, LANES), bf_pad, tri)
        fc = fc.reshape(batch, N_HEADS * (seq // FOX_TILE), FOX_TILE)

        a_sb = _sb_attention(qkv, uu, ATTN_TILE).reshape(n_tok, BRANCH_WIDTH)
        a_fx = _fox_attention(qkv, fc, FOX_TILE).reshape(n_tok, BRANCH_WIDTH)

        x2 = _post(x2, sh1, sc1, g1, a_sb, a_fx,
                   w_in[l][:, off_bgate:].astype(bf16), b_gate[l][None, :],
                   w_sb_out[l].astype(bf16), w_fox_out[l].astype(bf16), w_o[l].astype(bf16),
                   ln1_g[l][None, :], ln1_b[l][None, :], seq, alpha)
        x2 = _ffn(x2, sh2, sc2, g2,
                  w_ffn_gate[l].astype(bf16), w_ffn_up[l].astype(bf16), w_ffn_down[l].astype(bf16),
                  ln2_g[l][None, :], ln2_b[l][None, :], seq, alpha)
    return x2.reshape(batch, seq, d)
```

```python
import functools

import jax
import jax.numpy as jnp
from jax import lax
from jax.experimental import pallas as pl
from jax.experimental.pallas import tpu as pltpu

HEAD_DIM = 64
N_HEADS = 8
BRANCH_WIDTH = N_HEADS * HEAD_DIM
LANES = 128
HEADS_PER_BLOCK = LANES // HEAD_DIM
LN_EPS = 1e-5
NEG_BIG = -1e30
LOG2E = 1.4426950408889634
VMEM_LIMIT = 56 * 1024 * 1024
SB_UNDERFLOW_LOG = -105.0

ATTN_TILE = 256
FOX_TILE = 512
ROW_TILE = 512
FFN_CHUNK = 256
ADA_COLS = 1024

f32 = jnp.float32
bf16 = jnp.bfloat16


def _dot(a, b):
    return jnp.dot(a, b, preferred_element_type=f32)


def _dot_nt(a, b):
    return lax.dot_general(a, b, (((1,), (1,)), ((), ())), preferred_element_type=f32)


def _layer_norm(x):
    mu = jnp.mean(x, axis=-1, keepdims=True)
    xc = x - mu
    var = jnp.mean(xc * xc, axis=-1, keepdims=True)
    return xc * lax.rsqrt(var + LN_EPS)


def _sigmoid(x):
    return 1.0 / (1.0 + jnp.exp(-x))


def _params(*sem):
    return pltpu.CompilerParams(dimension_semantics=sem, vmem_limit_bytes=VMEM_LIMIT)


def _ada_kernel(c_ref, w_ref, b_ref, o_ref):
    c = c_ref[...]
    c_act = (c * _sigmoid(c)).astype(bf16)
    o_ref[...] = _dot(c_act, w_ref[...].astype(bf16)) + b_ref[...]


def _ada(c_pad, w, b):
    rows, d = c_pad.shape
    n = w.shape[1]
    return pl.pallas_call(
        _ada_kernel,
        grid=(n // ADA_COLS,),
        in_specs=[
            pl.BlockSpec((rows, d), lambda j: (0, 0)),
            pl.BlockSpec((d, ADA_COLS), lambda j: (0, j)),
            pl.BlockSpec((1, ADA_COLS), lambda j: (0, j)),
        ],
        out_specs=pl.BlockSpec((rows, ADA_COLS), lambda j: (0, j)),
        out_shape=jax.ShapeDtypeStruct((rows, n), f32),
        compiler_params=_params("arbitrary"),
        name="ada",
    )(c_pad, w, b)


def _inproj_kernel(x_ref, sh_ref, sc_ref, wqkv_ref, wf_ref, qkv_ref, fl_ref, *, col_chunk):
    u = (_layer_norm(x_ref[...]) * (1.0 + sc_ref[0]) + sh_ref[0]).astype(bf16)
    n = wqkv_ref.shape[1]
    for c0 in range(0, n, col_chunk):
        qkv_ref[:, c0:c0 + col_chunk] = _dot(u, wqkv_ref[:, c0:c0 + col_chunk]).astype(bf16)
    fl_ref[...] = _dot(u, wf_ref[...])


def _inproj(x2, sh, sc, wqkv, wf, seq):
    n_tok, d = x2.shape
    tiles_per_seq = seq // ROW_TILE
    bmap = lambda r: (r // tiles_per_seq, 0, 0)
    return pl.pallas_call(
        functools.partial(_inproj_kernel, col_chunk=512),
        grid=(n_tok // ROW_TILE,),
        in_specs=[
            pl.BlockSpec((ROW_TILE, d), lambda r: (r, 0)),
            pl.BlockSpec((1, 1, d), bmap),
            pl.BlockSpec((1, 1, d), bmap),
            pl.BlockSpec(wqkv.shape, lambda r: (0, 0)),
            pl.BlockSpec(wf.shape, lambda r: (0, 0)),
        ],
        out_specs=[
            pl.BlockSpec((ROW_TILE, wqkv.shape[1]), lambda r: (r, 0)),
            pl.BlockSpec((ROW_TILE, LANES), lambda r: (r, 0)),
        ],
        out_shape=[
            jax.ShapeDtypeStruct((n_tok, wqkv.shape[1]), bf16),
            jax.ShapeDtypeStruct((n_tok, LANES), f32),
        ],
        compiler_params=_params("arbitrary"),
        name="inproj",
    )(x2, sh, sc, wqkv, wf)


def _split3(x):
    hi = x.astype(bf16)
    r1 = x - hi.astype(f32)
    mid = r1.astype(bf16)
    lo = (r1 - mid.astype(f32)).astype(bf16)
    return hi, mid, lo


def _fcum_kernel(fl_ref, bf_ref, tri_ref, o_ref, *, chunk):
    seq = fl_ref.shape[1]
    tri = tri_ref[...]
    carry = jnp.zeros((N_HEADS, 1), f32)
    for c0 in range(0, seq, chunk):
        logit = fl_ref[0, c0:c0 + chunk, :] + bf_ref[...]
        ls = jnp.minimum(logit, 0.0) - jnp.log1p(jnp.exp(-jnp.abs(logit)))
        ls_t = ls.T[0:N_HEADS, :]
        hi, mid, lo = _split3(ls_t)
        cs = _dot(hi, tri) + _dot(mid, tri) + _dot(lo, tri) + carry
        o_ref[0, :, c0:c0 + chunk] = cs
        carry = cs[:, chunk - 1:chunk]


def _fcum(fl3, bf_pad, tri):
    b, seq, _ = fl3.shape
    chunk = tri.shape[0]
    return pl.pallas_call(
        functools.partial(_fcum_kernel, chunk=chunk),
        grid=(b,),
        in_specs=[
            pl.BlockSpec((1, seq, LANES), lambda i: (i, 0, 0)),
            pl.BlockSpec((1, LANES), lambda i: (0, 0)),
            pl.BlockSpec(tri.shape, lambda i: (0, 0)),
        ],
        out_specs=pl.BlockSpec((1, N_HEADS, seq), lambda i: (i, 0, 0)),
        out_shape=jax.ShapeDtypeStruct((b, N_HEADS, seq), f32),
        compiler_params=_params("arbitrary"),
        name="fcum",
    )(fl3, bf_pad, tri)


def _head_queries(q):
    lane = lax.broadcasted_iota(jnp.int32, q.shape, 1)
    zero = jnp.zeros_like(q)
    return [jnp.where(lane < HEAD_DIM, q, zero), jnp.where(lane >= HEAD_DIM, q, zero)]


def _lane_tile(x, width):
    reps = width // LANES
    return x if reps == 1 else jnp.concatenate([x] * reps, axis=1)


def _merge_heads(o0, o1):
    lane = lax.broadcasted_iota(jnp.int32, o0.shape, 1)
    return jnp.where(lane < HEAD_DIM, o0, o1)


def _sb_suffix(u, mask, uu):
    log_1m = jnp.minimum(u, 0.0) - jnp.log(1.0 + jnp.exp(-jnp.abs(u)))
    if mask is not None:
        log_1m = jnp.where(mask, log_1m, 0.0)
    hi = log_1m.astype(bf16)
    lo = (log_1m - hi.astype(f32)).astype(bf16)
    return _dot(jnp.concatenate([hi, lo], axis=1), uu)


def _sb_total(incl):
    return jnp.broadcast_to(incl[:, 0:1], (incl.shape[0], LANES))


def _sb_weights(u, incl, carry, mask):
    g = incl if carry is None else incl + _lane_tile(carry, u.shape[1])
    a = jnp.exp(g - u)
    if mask is not None:
        a = jnp.where(mask, a, 0.0)
    return a.astype(bf16)


def _sb_block(u, carry, mask, uu):
    incl = _sb_suffix(u, mask, uu)
    return _sb_weights(u, incl, carry, mask), _sb_total(incl)


def _sb_kernel(q_ref, k_ref, v_ref, uu_ref, o_ref, acc_ref, car_ref, *, tile):
    i = pl.program_id(2)
    qh = _head_queries(-q_ref[0])
    uu = uu_ref[...]
    row = lax.broadcasted_iota(jnp.int32, (tile, tile), 0)
    col = lax.broadcasted_iota(jnp.int32, (tile, tile), 1)
    below = col < row

    @pl.when(i == 0)
    def _():
        k = k_ref[0, 0:tile, :]
        v = v_ref[0, 0:tile, :]
        for h in range(HEADS_PER_BLOCK):
            a, total = _sb_block(_dot_nt(qh[h], k), None, below, uu)
            acc_ref[h] = _dot(a, v)
            car_ref[h] = total

    @pl.when(i > 0)
    def _():
        start = pl.multiple_of((i - 1) * tile, tile)
        k = k_ref[0, pl.ds(start, 2 * tile), :]
        v = v_ref[0, pl.ds(start, 2 * tile), :]
        heads = range(HEADS_PER_BLOCK)
        u = [_dot_nt(qh[h], k) for h in heads]
        s_diag = [_sb_suffix(u[h][:, tile:], below, uu) for h in heads]
        s_prev = [_sb_suffix(u[h][:, :tile], None, uu) for h in heads]
        for h in heads:
            t_diag = _sb_total(s_diag[h])
            a_diag = _sb_weights(u[h][:, tile:], s_diag[h], None, below)
            a_prev = _sb_weights(u[h][:, :tile], s_prev[h], t_diag, None)
            acc_ref[h] = _dot(jnp.concatenate([a_prev, a_diag], axis=1), v)
            car_ref[h] = t_diag + _sb_total(s_prev[h])

    def exhausted():
        return jnp.max(jnp.maximum(car_ref[0], car_ref[1])) < SB_UNDERFLOW_LOG

    def cond(state):
        j, done = state
        return jnp.logical_and(j >= 0, jnp.logical_not(done))

    def body(state):
        j, _ = state
        start = pl.multiple_of(j * tile, tile)
        k = k_ref[0, pl.ds(start, tile), :]
        v = v_ref[0, pl.ds(start, tile), :]
        for h in range(HEADS_PER_BLOCK):
            a, total = _sb_block(_dot_nt(qh[h], k), car_ref[h], None, uu)
            acc_ref[h] += _dot(a, v)
            car_ref[h] += total
        return j - 1, exhausted()

    lax.while_loop(cond, body, (i - 2, exhausted()))
    o_ref[0] = _merge_heads(acc_ref[0], acc_ref[1]).astype(o_ref.dtype)


def _sb_attention(qkv, uu, tile):
    b, seq, _ = qkv.shape
    blocks = BRANCH_WIDTH // LANES
    return pl.pallas_call(
        functools.partial(_sb_kernel, tile=tile),
        grid=(b, blocks, seq // tile),
        in_specs=[
            pl.BlockSpec((1, tile, LANES), lambda bi, p, i: (bi, i, p)),
            pl.BlockSpec((1, seq, LANES), lambda bi, p, i: (bi, 0, blocks + p)),
            pl.BlockSpec((1, seq, LANES), lambda bi, p, i: (bi, 0, 2 * blocks + p)),
            pl.BlockSpec(uu.shape, lambda bi, p, i: (0, 0)),
        ],
        out_specs=pl.BlockSpec((1, tile, LANES), lambda bi, p, i: (bi, i, p)),
        out_shape=jax.ShapeDtypeStruct((b, seq, BRANCH_WIDTH), bf16),
        scratch_shapes=[
            pltpu.VMEM((HEADS_PER_BLOCK, tile, LANES), f32),
            pltpu.VMEM((HEADS_PER_BLOCK, tile, LANES), f32),
        ],
        compiler_params=_params("parallel", "parallel", "arbitrary"),
        name="sb_attn",
    )(qkv, qkv, qkv, uu)


def _lane_fold(x, op):
    return functools.reduce(op, [x[:, t * LANES:(t + 1) * LANES] for t in range(x.shape[1] // LANES)])


def _fox_kernel(qa_ref, qb_ref, k_ref, v_ref, fc_ref, oa_ref, ob_ref, qh_ref, s_ref, m_ref, acc_ref,
                *, tile, n_blocks):
    p = pl.program_id(1)
    t = pl.program_id(2)
    n_slots = n_blocks + 1
    row = lax.broadcasted_iota(jnp.int32, (tile, tile), 0)
    col = lax.broadcasted_iota(jnp.int32, (tile, tile), 1)
    causal = col <= row

    for w, q_ref in enumerate((qa_ref, qb_ref)):
        for h, q_head in enumerate(_head_queries(q_ref[0])):
            qh_ref[w, h] = q_head
    m_ref[...] = jnp.full_like(m_ref, NEG_BIG)
    acc_ref[...] = jnp.zeros_like(acc_ref)

    def owner(c):
        if c == 0:
            return 0, t
        if c == n_slots - 1:
            return 1, n_blocks - 1 - t
        is_b = c > t
        return is_b.astype(jnp.int32), jnp.where(is_b, c - t - 1, t - c)

    for c in range(n_slots):
        w, kc = owner(c)
        k = k_ref[0, pl.ds(pl.multiple_of(kc * tile, tile), tile), :]
        for h in range(HEADS_PER_BLOCK):
            head = p * HEADS_PER_BLOCK + h
            f_key = fc_ref[0, pl.ds(head * n_blocks + kc, 1), :]
            s = _dot_nt(qh_ref[w, h], k) * LOG2E - f_key * LOG2E
            if c in (0, n_slots - 1):
                s = jnp.where(causal, s, NEG_BIG)
            m_ref[w, h] = jnp.maximum(m_ref[w, h], _lane_fold(s, jnp.maximum))
            s_ref[h, c] = s

    for w in range(2):
        for h in range(HEADS_PER_BLOCK):
            m_ref[w, h] = jnp.broadcast_to(jnp.max(m_ref[w, h], axis=1, keepdims=True), (tile, LANES))

    ones = jnp.ones((tile, LANES), bf16)
    for c in range(n_slots):
        w, kc = owner(c)
        v = jnp.concatenate([v_ref[0, pl.ds(pl.multiple_of(kc * tile, tile), tile), :], ones], axis=1)
        for h in range(HEADS_PER_BLOCK):
            pr = jnp.exp2(s_ref[h, c] - _lane_tile(m_ref[w, h], tile))
            acc_ref[w, h] += _dot(pr.astype(bf16), v)

    for w, o_ref in enumerate((oa_ref, ob_ref)):
        out = [acc_ref[w, h, :, 0:LANES] / acc_ref[w, h, :, LANES:2 * LANES] for h in range(HEADS_PER_BLOCK)]
        o_ref[0] = _merge_heads(out[0], out[1]).astype(o_ref.dtype)


def _fox_attention(qkv, fc, tile):
    b, seq, _ = qkv.shape
    blocks = BRANCH_WIDTH // LANES
    n_blocks = seq // tile
    half = n_blocks // 2
    base = 3 * blocks
    out = jax.ShapeDtypeStruct((b, seq // 2, BRANCH_WIDTH), bf16)
    return pl.pallas_call(
        functools.partial(_fox_kernel, tile=tile, n_blocks=n_blocks),
        grid=(b, blocks, half),
        in_specs=[
            pl.BlockSpec((1, tile, LANES), lambda bi, p, t: (bi, t, base + p)),
            pl.BlockSpec((1, tile, LANES), lambda bi, p, t: (bi, n_blocks - 1 - t, base + p)),
            pl.BlockSpec((1, seq, LANES), lambda bi, p, t: (bi, 0, base + blocks + p)),
            pl.BlockSpec((1, seq, LANES), lambda bi, p, t: (bi, 0, base + 2 * blocks + p)),
            pl.BlockSpec((1, N_HEADS * n_blocks, tile), lambda bi, p, t: (bi, 0, 0)),
        ],
        out_specs=[
            pl.BlockSpec((1, tile, LANES), lambda bi, p, t: (bi, t, p)),
            pl.BlockSpec((1, tile, LANES), lambda bi, p, t: (bi, half - 1 - t, p)),
        ],
        out_shape=[out, out],
        scratch_shapes=[
            pltpu.VMEM((2, HEADS_PER_BLOCK, tile, LANES), bf16),
            pltpu.VMEM((HEADS_PER_BLOCK, n_blocks + 1, tile, tile), f32),
            pltpu.VMEM((2, HEADS_PER_BLOCK, tile, LANES), f32),
            pltpu.VMEM((2, HEADS_PER_BLOCK, tile, 2 * LANES), f32),
        ],
        compiler_params=_params("parallel", "parallel", "arbitrary"),
        name="fox_attn",
    )(qkv, qkv, qkv, qkv, fc)


def _post_kernel(x_ref, sh_ref, sc_ref, g1_ref, asb_ref, afx_lo_ref, afx_hi_ref, wg_ref, bg_ref, wsb_ref,
                 wfx_ref, wo_ref, lg_ref, lb_ref, o_ref, *, alpha, tiles_per_seq):
    x = x_ref[...]
    d = x.shape[1]
    u = (_layer_norm(x) * (1.0 + sc_ref[0]) + sh_ref[0]).astype(bf16)
    y_sb = _dot(asb_ref[...], wsb_ref[...])
    in_first_half = (pl.program_id(0) % tiles_per_seq) < tiles_per_seq // 2
    a_fx = jnp.where(in_first_half, afx_lo_ref[...], afx_hi_ref[...])
    y_fx = _dot(a_fx, wfx_ref[...])
    g_sb = _sigmoid(_dot(u, wg_ref[:, 0:d]) + bg_ref[:, 0:d])
    g_fx = _sigmoid(_dot(u, wg_ref[:, d:2 * d]) + bg_ref[:, d:2 * d])
    mixed = (g_sb * y_sb + g_fx * y_fx).astype(bf16)
    mix = _dot(mixed, wo_ref[...])
    r = alpha * x + g1_ref[0] * mix
    o_ref[...] = _layer_norm(r) * lg_ref[...] + lb_ref[...]


def _post(x2, sh, sc, g1, asb, afx_lo, afx_hi, wg, bg, wsb, wfx, wo, lg, lb, seq, alpha):
    n_tok, d = x2.shape
    tile = ROW_TILE // 2
    tiles_per_seq = seq // tile
    half = tiles_per_seq // 2
    bmap = lambda r: (r // tiles_per_seq, 0, 0)
    lo_map = lambda r: ((r // tiles_per_seq) * half + jnp.minimum(r % tiles_per_seq, half - 1), 0)
    hi_map = lambda r: ((r // tiles_per_seq) * half + jnp.maximum(r % tiles_per_seq - half, 0), 0)
    full = lambda a: pl.BlockSpec(a.shape, lambda r: (0,) * a.ndim)
    return pl.pallas_call(
        functools.partial(_post_kernel, alpha=alpha, tiles_per_seq=tiles_per_seq),
        grid=(n_tok // tile,),
        in_specs=[
            pl.BlockSpec((tile, d), lambda r: (r, 0)),
            pl.BlockSpec((1, 1, d), bmap),
            pl.BlockSpec((1, 1, d), bmap),
            pl.BlockSpec((1, 1, d), bmap),
            pl.BlockSpec((tile, BRANCH_WIDTH), lambda r: (r, 0)),
            pl.BlockSpec((tile, BRANCH_WIDTH), lo_map),
            pl.BlockSpec((tile, BRANCH_WIDTH), hi_map),
            full(wg), full(bg), full(wsb), full(wfx), full(wo), full(lg), full(lb),
        ],
        out_specs=pl.BlockSpec((tile, d), lambda r: (r, 0)),
        out_shape=jax.ShapeDtypeStruct((n_tok, d), f32),
        compiler_params=_params("arbitrary"),
        name="post",
    )(x2, sh, sc, g1, asb, afx_lo, afx_hi, wg, bg, wsb, wfx, wo, lg, lb)


def _ffn_kernel(x_ref, sh_ref, sc_ref, g2_ref, wg_ref, wu_ref, wd_ref, lg_ref, lb_ref, o_ref, acc_ref,
                *, alpha):
    x = x_ref[...]
    u = (_layer_norm(x) * (1.0 + sc_ref[0]) + sh_ref[0]).astype(bf16)
    d_ff = wg_ref.shape[1]
    for c0 in range(0, d_ff, FFN_CHUNK):
        gate = _dot(u, wg_ref[:, c0:c0 + FFN_CHUNK])
        up = _dot(u, wu_ref[:, c0:c0 + FFN_CHUNK])
        h = (gate * _sigmoid(gate) * up).astype(bf16)
        part = _dot(h, wd_ref[c0:c0 + FFN_CHUNK, :])
        if c0 == 0:
            acc_ref[...] = part
        else:
            acc_ref[...] += part
    r = alpha * x + g2_ref[0] * acc_ref[...]
    o_ref[...] = _layer_norm(r) * lg_ref[...] + lb_ref[...]


def _ffn(x2, sh, sc, g2, wg, wu, wd, lg, lb, seq, alpha):
    n_tok, d = x2.shape
    tile = ROW_TILE
    tiles_per_seq = seq // tile
    bmap = lambda r: (r // tiles_per_seq, 0, 0)
    resident = lambda a: pl.BlockSpec(a.shape, lambda r: (0,) * a.ndim, pipeline_mode=pl.Buffered(1))
    return pl.pallas_call(
        functools.partial(_ffn_kernel, alpha=alpha),
        grid=(n_tok // tile,),
        in_specs=[
            pl.BlockSpec((tile, d), lambda r: (r, 0)),
            pl.BlockSpec((1, 1, d), bmap),
            pl.BlockSpec((1, 1, d), bmap),
            pl.BlockSpec((1, 1, d), bmap),
            resident(wg), resident(wu), resident(wd), resident(lg), resident(lb),
        ],
        out_specs=pl.BlockSpec((tile, d), lambda r: (r, 0)),
        out_shape=jax.ShapeDtypeStruct((n_tok, d), f32),
        scratch_shapes=[pltpu.VMEM((tile, d), f32)],
        compiler_params=_params("arbitrary"),
        name="ffn",
    )(x2, sh, sc, g2, wg, wu, wd, lg, lb)


def _suffix_sum_pair(tile):
    r = jnp.arange(tile)[:, None]
    s = jnp.arange(tile)[None, :]
    u = (r >= s).astype(bf16)
    return jnp.concatenate([u, u], axis=0)


def _inclusive_upper(chunk):
    r = jnp.arange(chunk)[:, None]
    s = jnp.arange(chunk)[None, :]
    return (r <= s).astype(bf16)


def kernel(x, c, w_ada, b_ada, w_in, b_gate, b_forget, w_sb_out, w_fox_out, w_o, ln1_g, ln1_b,
           w_ffn_gate, w_ffn_up, w_ffn_down, ln2_g, ln2_b):
    batch, seq, d = x.shape
    depth = w_ada.shape[0]
    alpha = (2 * depth) ** 0.25
    n_tok = batch * seq
    qkv_cols = 6 * BRANCH_WIDTH
    off_fgate = qkv_cols
    off_bgate = off_fgate + N_HEADS
    assert w_in.shape[2] == off_bgate + 2 * d
    assert seq % ATTN_TILE == 0 and seq % FOX_TILE == 0 and seq % ROW_TILE == 0

    c_pad = jnp.zeros((8, d), f32).at[:batch].set(c)
    uu = _suffix_sum_pair(ATTN_TILE)
    tri = _inclusive_upper(256)
    col_scale = jnp.ones((qkv_cols,), f32)
    col_scale = col_scale.at[0:BRANCH_WIDTH].set(HEAD_DIM ** -0.5)
    col_scale = col_scale.at[3 * BRANCH_WIDTH:4 * BRANCH_WIDTH].set(HEAD_DIM ** -0.5)

    x2 = x.reshape(n_tok, d)
    for l in range(depth):
        ada = _ada(c_pad, w_ada[l], b_ada[l][None, :])[:batch]
        sh1, sc1, g1, sh2, sc2, g2 = [t[:, None, :] for t in jnp.split(ada, 6, axis=-1)]

        wqkv = (w_in[l][:, :qkv_cols] * col_scale).astype(bf16)
        wf = jnp.zeros((d, LANES), f32).at[:, :N_HEADS].set(w_in[l][:, off_fgate:off_bgate]).astype(bf16)
        bf_pad = jnp.zeros((1, LANES), f32).at[0, :N_HEADS].set(b_forget[l])

        qkv, flog = _inproj(x2, sh1, sc1, wqkv, wf, seq)
        qkv = qkv.reshape(batch, seq, qkv_cols)
        fc = _fcum(flog.reshape(batch, seq, LANES), bf_pad, tri)
        fc = fc.reshape(batch, N_HEADS * (seq // FOX_TILE), FOX_TILE)

        a_sb = _sb_attention(qkv, uu, ATTN_TILE).reshape(n_tok, BRANCH_WIDTH)
        a_fx_lo, a_fx_hi = [a.reshape(n_tok // 2, BRANCH_WIDTH) for a in _fox_attention(qkv, fc, FOX_TILE)]

        x2 = _post(x2, sh1, sc1, g1, a_sb, a_fx_lo, a_fx_hi,
                   w_in[l][:, off_bgate:].astype(bf16), b_gate[l][None, :],
                   w_sb_out[l].astype(bf16), w_fox_out[l].astype(bf16), w_o[l].astype(bf16),
                   ln1_g[l][None, :], ln1_b[l][None, :], seq, alpha)
        x2 = _ffn(x2, sh2, sc2, g2,
                  w_ffn_gate[l].astype(bf16), w_ffn_up[l].astype(bf16), w_ffn_down[l].astype(bf16),
                  ln2_g[l][None, :], ln2_b[l][None, :], seq, alpha)
    return x2.reshape(batch, seq, d)
```

```python
import functools

import jax
import jax.numpy as jnp
from jax import lax
from jax.experimental import pallas as pl
from jax.experimental.pallas import tpu as pltpu

HEAD_DIM = 64
N_HEADS = 8
BRANCH_WIDTH = N_HEADS * HEAD_DIM
LANES = 128
HEADS_PER_BLOCK = LANES // HEAD_DIM
LN_EPS = 1e-5
NEG_BIG = -1e30
LOG2E = 1.4426950408889634
VMEM_LIMIT = 56 * 1024 * 1024
SB_UNDERFLOW_LOG = -105.0

ATTN_TILE = 256
SB_LANE_BLOCKS = 2
FOX_TILE = 512
ROW_TILE = 512
FFN_CHUNK = 256
ADA_COLS = 1024

f32 = jnp.float32
bf16 = jnp.bfloat16


def _dot(a, b):
    return jnp.dot(a, b, preferred_element_type=f32)


def _dot_nt(a, b):
    return lax.dot_general(a, b, (((1,), (1,)), ((), ())), preferred_element_type=f32)


def _layer_norm(x):
    mu = jnp.mean(x, axis=-1, keepdims=True)
    xc = x - mu
    var = jnp.mean(xc * xc, axis=-1, keepdims=True)
    return xc * lax.rsqrt(var + LN_EPS)


def _sigmoid(x):
    return 1.0 / (1.0 + jnp.exp(-x))


def _params(*sem):
    return pltpu.CompilerParams(dimension_semantics=sem, vmem_limit_bytes=VMEM_LIMIT)


def _ada_kernel(c_ref, w_ref, b_ref, o_ref):
    c = c_ref[...]
    c_act = (c * _sigmoid(c)).astype(bf16)
    o_ref[...] = _dot(c_act, w_ref[...].astype(bf16)) + b_ref[...]


def _ada(c_pad, w, b):
    rows, d = c_pad.shape
    n = w.shape[1]
    return pl.pallas_call(
        _ada_kernel,
        grid=(n // ADA_COLS,),
        in_specs=[
            pl.BlockSpec((rows, d), lambda j: (0, 0)),
            pl.BlockSpec((d, ADA_COLS), lambda j: (0, j)),
            pl.BlockSpec((1, ADA_COLS), lambda j: (0, j)),
        ],
        out_specs=pl.BlockSpec((rows, ADA_COLS), lambda j: (0, j)),
        out_shape=jax.ShapeDtypeStruct((rows, n), f32),
        compiler_params=_params("arbitrary"),
        name="ada",
    )(c_pad, w, b)


def _inproj_kernel(x_ref, sh_ref, sc_ref, wqkv_ref, wf_ref, qkv_ref, fl_ref, *, col_chunk):
    u = (_layer_norm(x_ref[...]) * (1.0 + sc_ref[0]) + sh_ref[0]).astype(bf16)
    n = wqkv_ref.shape[0]
    for c0 in range(0, n, col_chunk):
        qkv_ref[:, c0:c0 + col_chunk] = _dot_nt(u, wqkv_ref[c0:c0 + col_chunk, :]).astype(bf16)
    fl_ref[...] = _dot_nt(u, wf_ref[...])


def _inproj(x2, sh, sc, wqkv, wf, seq):
    n_tok, d = x2.shape
    tiles_per_seq = seq // ROW_TILE
    bmap = lambda r: (r // tiles_per_seq, 0, 0)
    return pl.pallas_call(
        functools.partial(_inproj_kernel, col_chunk=512),
        grid=(n_tok // ROW_TILE,),
        in_specs=[
            pl.BlockSpec((ROW_TILE, d), lambda r: (r, 0)),
            pl.BlockSpec((1, 1, d), bmap),
            pl.BlockSpec((1, 1, d), bmap),
            pl.BlockSpec(wqkv.shape, lambda r: (0, 0)),
            pl.BlockSpec(wf.shape, lambda r: (0, 0)),
        ],
        out_specs=[
            pl.BlockSpec((ROW_TILE, wqkv.shape[0]), lambda r: (r, 0)),
            pl.BlockSpec((ROW_TILE, LANES), lambda r: (r, 0)),
        ],
        out_shape=[
            jax.ShapeDtypeStruct((n_tok, wqkv.shape[0]), bf16),
            jax.ShapeDtypeStruct((n_tok, LANES), f32),
        ],
        compiler_params=_params("arbitrary"),
        name="inproj",
    )(x2, sh, sc, wqkv, wf)


def _split3(x):
    hi = x.astype(bf16)
    r1 = x - hi.astype(f32)
    mid = r1.astype(bf16)
    lo = (r1 - mid.astype(f32)).astype(bf16)
    return hi, mid, lo


def _fcum_kernel(fl_ref, bf_ref, tri_ref, o_ref, *, chunk):
    seq = fl_ref.shape[1]
    tri = tri_ref[...]
    carry = jnp.zeros((N_HEADS, 1), f32)
    for c0 in range(0, seq, chunk):
        logit = fl_ref[0, c0:c0 + chunk, :] + bf_ref[...]
        ls = jnp.minimum(logit, 0.0) - jnp.log1p(jnp.exp(-jnp.abs(logit)))
        ls_t = ls.T[0:N_HEADS, :]
        hi, mid, lo = _split3(ls_t)
        cs = _dot(hi, tri) + _dot(mid, tri) + _dot(lo, tri) + carry
        o_ref[0, :, c0:c0 + chunk] = cs
        carry = cs[:, chunk - 1:chunk]


def _fcum(fl3, bf_pad, tri):
    b, seq, _ = fl3.shape
    chunk = tri.shape[0]
    return pl.pallas_call(
        functools.partial(_fcum_kernel, chunk=chunk),
        grid=(b,),
        in_specs=[
            pl.BlockSpec((1, seq, LANES), lambda i: (i, 0, 0)),
            pl.BlockSpec((1, LANES), lambda i: (0, 0)),
            pl.BlockSpec(tri.shape, lambda i: (0, 0)),
        ],
        out_specs=pl.BlockSpec((1, N_HEADS, seq), lambda i: (i, 0, 0)),
        out_shape=jax.ShapeDtypeStruct((b, N_HEADS, seq), f32),
        compiler_params=_params("arbitrary"),
        name="fcum",
    )(fl3, bf_pad, tri)


def _head_queries(q):
    lane = lax.broadcasted_iota(jnp.int32, q.shape, 1)
    zero = jnp.zeros_like(q)
    return [jnp.where(lane < HEAD_DIM, q, zero), jnp.where(lane >= HEAD_DIM, q, zero)]


def _lane_tile(x, width):
    reps = width // LANES
    return x if reps == 1 else jnp.concatenate([x] * reps, axis=1)


def _merge_heads(o0, o1):
    lane = lax.broadcasted_iota(jnp.int32, o0.shape, 1)
    return jnp.where(lane < HEAD_DIM, o0, o1)


def _sb_suffix(u, mask, uu):
    log_1m = jnp.minimum(u, 0.0) - jnp.log(1.0 + jnp.exp(-jnp.abs(u)))
    if mask is not None:
        log_1m = jnp.where(mask, log_1m, 0.0)
    hi = log_1m.astype(bf16)
    lo = (log_1m - hi.astype(f32)).astype(bf16)
    return _dot(jnp.concatenate([hi, lo], axis=1), uu)


def _sb_total(incl):
    return jnp.broadcast_to(incl[:, 0:1], (incl.shape[0], LANES))


def _sb_weights(u, incl, carry, mask):
    g = incl if carry is None else incl + _lane_tile(carry, u.shape[1])
    a = jnp.exp(g - u)
    if mask is not None:
        a = jnp.where(mask, a, 0.0)
    return a.astype(bf16)


def _sb_block(u, carry, mask, uu):
    incl = _sb_suffix(u, mask, uu)
    return _sb_weights(u, incl, carry, mask), _sb_total(incl)


def _sb_kernel(q_ref, k_ref, v_ref, uu_ref, o_ref, acc_ref, car_ref, *, tile, lane_blocks):
    i = pl.program_id(2)
    n_heads = lane_blocks * HEADS_PER_BLOCK
    heads = range(n_heads)
    uu = uu_ref[...]
    row = lax.broadcasted_iota(jnp.int32, (tile, tile), 0)
    col = lax.broadcasted_iota(jnp.int32, (tile, tile), 1)
    below = col < row

    def lanes_of(h):
        lb = h // HEADS_PER_BLOCK
        return slice(lb * LANES, (lb + 1) * LANES)

    qh = []
    for lb in range(lane_blocks):
        qh += _head_queries(-q_ref[0, :, lb * LANES:(lb + 1) * LANES])

    @pl.when(i == 0)
    def _():
        for h in heads:
            a, total = _sb_block(_dot_nt(qh[h], k_ref[0, 0:tile, lanes_of(h)]), None, below, uu)
            acc_ref[h] = _dot(a, v_ref[0, 0:tile, lanes_of(h)])
            car_ref[h] = total

    @pl.when(i > 0)
    def _():
        rows = pl.ds(pl.multiple_of((i - 1) * tile, tile), 2 * tile)
        u = [_dot_nt(qh[h], k_ref[0, rows, lanes_of(h)]) for h in heads]
        s_diag = [_sb_suffix(u[h][:, tile:], below, uu) for h in heads]
        s_prev = [_sb_suffix(u[h][:, :tile], None, uu) for h in heads]
        for h in heads:
            t_diag = _sb_total(s_diag[h])
            a_diag = _sb_weights(u[h][:, tile:], s_diag[h], None, below)
            a_prev = _sb_weights(u[h][:, :tile], s_prev[h], t_diag, None)
            acc_ref[h] = _dot(jnp.concatenate([a_prev, a_diag], axis=1), v_ref[0, rows, lanes_of(h)])
            car_ref[h] = t_diag + _sb_total(s_prev[h])

    def exhausted():
        worst = functools.reduce(jnp.maximum, [car_ref[h] for h in heads])
        return jnp.max(worst) < SB_UNDERFLOW_LOG

    def cond(state):
        j, done = state
        return jnp.logical_and(j >= 0, jnp.logical_not(done))

    def body(state):
        j, _ = state
        rows = pl.ds(pl.multiple_of(j * tile, tile), tile)
        for h in heads:
            a, total = _sb_block(_dot_nt(qh[h], k_ref[0, rows, lanes_of(h)]), car_ref[h], None, uu)
            acc_ref[h] += _dot(a, v_ref[0, rows, lanes_of(h)])
            car_ref[h] += total
        return j - 1, exhausted()

    lax.while_loop(cond, body, (i - 2, exhausted()))
    for lb in range(lane_blocks):
        h0 = lb * HEADS_PER_BLOCK
        o_ref[0, :, lb * LANES:(lb + 1) * LANES] = _merge_heads(acc_ref[h0], acc_ref[h0 + 1]).astype(o_ref.dtype)


def _sb_attention(qkv, uu, tile, lane_blocks):
    b, seq, _ = qkv.shape
    width = lane_blocks * LANES
    blocks = BRANCH_WIDTH // width
    n_heads = lane_blocks * HEADS_PER_BLOCK
    return pl.pallas_call(
        functools.partial(_sb_kernel, tile=tile, lane_blocks=lane_blocks),
        grid=(b, blocks, seq // tile),
        in_specs=[
            pl.BlockSpec((1, tile, width), lambda bi, p, i: (bi, i, p)),
            pl.BlockSpec((1, seq, width), lambda bi, p, i: (bi, 0, blocks + p)),
            pl.BlockSpec((1, seq, width), lambda bi, p, i: (bi, 0, 2 * blocks + p)),
            pl.BlockSpec(uu.shape, lambda bi, p, i: (0, 0)),
        ],
        out_specs=pl.BlockSpec((1, tile, width), lambda bi, p, i: (bi, i, p)),
        out_shape=jax.ShapeDtypeStruct((b, seq, BRANCH_WIDTH), bf16),
        scratch_shapes=[
            pltpu.VMEM((n_heads, tile, LANES), f32),
            pltpu.VMEM((n_heads, tile, LANES), f32),
        ],
        compiler_params=_params("parallel", "parallel", "arbitrary"),
        name="sb_attn",
    )(qkv, qkv, qkv, uu)


def _lane_fold(x, op):
    return functools.reduce(op, [x[:, t * LANES:(t + 1) * LANES] for t in range(x.shape[1] // LANES)])


def _fox_kernel(qa_ref, qb_ref, k_ref, v_ref, fc_ref, oa_ref, ob_ref, qh_ref, s_ref, m_ref, acc_ref,
                *, tile, n_blocks):
    p = pl.program_id(1)
    t = pl.program_id(2)
    n_slots = n_blocks + 1
    row = lax.broadcasted_iota(jnp.int32, (tile, tile), 0)
    col = lax.broadcasted_iota(jnp.int32, (tile, tile), 1)
    causal = col <= row

    for w, q_ref in enumerate((qa_ref, qb_ref)):
        for h, q_head in enumerate(_head_queries(q_ref[0])):
            qh_ref[w, h] = q_head
    m_ref[...] = jnp.full_like(m_ref, NEG_BIG)
    acc_ref[...] = jnp.zeros_like(acc_ref)

    def owner(c):
        if c == 0:
            return 0, t
        if c == n_slots - 1:
            return 1, n_blocks - 1 - t
        is_b = c > t
        return is_b.astype(jnp.int32), jnp.where(is_b, c - t - 1, t - c)

    for c in range(n_slots):
        w, kc = owner(c)
        k = k_ref[0, pl.ds(pl.multiple_of(kc * tile, tile), tile), :]
        for h in range(HEADS_PER_BLOCK):
            head = p * HEADS_PER_BLOCK + h
            f_key = fc_ref[0, pl.ds(head * n_blocks + kc, 1), :]
            s = _dot_nt(qh_ref[w, h], k) * LOG2E - f_key * LOG2E
            if c in (0, n_slots - 1):
                s = jnp.where(causal, s, NEG_BIG)
            m_ref[w, h] = jnp.maximum(m_ref[w, h], _lane_fold(s, jnp.maximum))
            s_ref[h, c] = s

    for w in range(2):
        for h in range(HEADS_PER_BLOCK):
            m_ref[w, h] = jnp.broadcast_to(jnp.max(m_ref[w, h], axis=1, keepdims=True), (tile, LANES))

    ones = jnp.ones((tile, LANES), bf16)
    for c in range(n_slots):
        w, kc = owner(c)
        v = jnp.concatenate([v_ref[0, pl.ds(pl.multiple_of(kc * tile, tile), tile), :], ones], axis=1)
        for h in range(HEADS_PER_BLOCK):
            pr = jnp.exp2(s_ref[h, c] - _lane_tile(m_ref[w, h], tile))
            acc_ref[w, h] += _dot(pr.astype(bf16), v)

    for w, o_ref in enumerate((oa_ref, ob_ref)):
        out = [acc_ref[w, h, :, 0:LANES] / acc_ref[w, h, :, LANES:2 * LANES] for h in range(HEADS_PER_BLOCK)]
        o_ref[0] = _merge_heads(out[0], out[1]).astype(o_ref.dtype)


def _fox_attention(qkv, fc, tile):
    b, seq, _ = qkv.shape
    blocks = BRANCH_WIDTH // LANES
    n_blocks = seq // tile
    half = n_blocks // 2
    base = 3 * blocks
    out = jax.ShapeDtypeStruct((b, seq // 2, BRANCH_WIDTH), bf16)
    return pl.pallas_call(
        functools.partial(_fox_kernel, tile=tile, n_blocks=n_blocks),
        grid=(b, blocks, half),
        in_specs=[
            pl.BlockSpec((1, tile, LANES), lambda bi, p, t: (bi, t, base + p)),
            pl.BlockSpec((1, tile, LANES), lambda bi, p, t: (bi, n_blocks - 1 - t, base + p)),
            pl.BlockSpec((1, seq, LANES), lambda bi, p, t: (bi, 0, base + blocks + p)),
            pl.BlockSpec((1, seq, LANES), lambda bi, p, t: (bi, 0, base + 2 * blocks + p)),
            pl.BlockSpec((1, N_HEADS * n_blocks, tile), lambda bi, p, t: (bi, 0, 0)),
        ],
        out_specs=[
            pl.BlockSpec((1, tile, LANES), lambda bi, p, t: (bi, t, p)),
            pl.BlockSpec((1, tile, LANES), lambda bi, p, t: (bi, half - 1 - t, p)),
        ],
        out_shape=[out, out],
        scratch_shapes=[
            pltpu.VMEM((2, HEADS_PER_BLOCK, tile, LANES), bf16),
            pltpu.VMEM((HEADS_PER_BLOCK, n_blocks + 1, tile, tile), f32),
            pltpu.VMEM((2, HEADS_PER_BLOCK, tile, LANES), f32),
            pltpu.VMEM((2, HEADS_PER_BLOCK, tile, 2 * LANES), f32),
        ],
        compiler_params=_params("parallel", "parallel", "arbitrary"),
        name="fox_attn",
    )(qkv, qkv, qkv, qkv, fc)


def _post_kernel(x_ref, sh_ref, sc_ref, g1_ref, asb_ref, afx_lo_ref, afx_hi_ref, wg_ref, bg_ref, wsb_ref,
                 wfx_ref, wo_ref, lg_ref, lb_ref, o_ref, *, alpha, tiles_per_seq):
    x = x_ref[...]
    d = x.shape[1]
    u = (_layer_norm(x) * (1.0 + sc_ref[0]) + sh_ref[0]).astype(bf16)
    y_sb = _dot(asb_ref[...], wsb_ref[...])
    in_first_half = (pl.program_id(0) % tiles_per_seq) < tiles_per_seq // 2
    a_fx = jnp.where(in_first_half, afx_lo_ref[...], afx_hi_ref[...])
    y_fx = _dot(a_fx, wfx_ref[...])
    g_sb = _sigmoid(_dot_nt(u, wg_ref[0:d, :]) + bg_ref[:, 0:d])
    g_fx = _sigmoid(_dot_nt(u, wg_ref[d:2 * d, :]) + bg_ref[:, d:2 * d])
    mixed = (g_sb * y_sb + g_fx * y_fx).astype(bf16)
    mix = _dot(mixed, wo_ref[...])
    r = alpha * x + g1_ref[0] * mix
    o_ref[...] = _layer_norm(r) * lg_ref[...] + lb_ref[...]


def _post(x2, sh, sc, g1, asb, afx_lo, afx_hi, wg, bg, wsb, wfx, wo, lg, lb, seq, alpha):
    n_tok, d = x2.shape
    tile = ROW_TILE // 2
    tiles_per_seq = seq // tile
    half = tiles_per_seq // 2
    bmap = lambda r: (r // tiles_per_seq, 0, 0)
    lo_map = lambda r: ((r // tiles_per_seq) * half + jnp.minimum(r % tiles_per_seq, half - 1), 0)
    hi_map = lambda r: ((r // tiles_per_seq) * half + jnp.maximum(r % tiles_per_seq - half, 0), 0)
    full = lambda a: pl.BlockSpec(a.shape, lambda r: (0,) * a.ndim)
    return pl.pallas_call(
        functools.partial(_post_kernel, alpha=alpha, tiles_per_seq=tiles_per_seq),
        grid=(n_tok // tile,),
        in_specs=[
            pl.BlockSpec((tile, d), lambda r: (r, 0)),
            pl.BlockSpec((1, 1, d), bmap),
            pl.BlockSpec((1, 1, d), bmap),
            pl.BlockSpec((1, 1, d), bmap),
            pl.BlockSpec((tile, BRANCH_WIDTH), lambda r: (r, 0)),
            pl.BlockSpec((tile, BRANCH_WIDTH), lo_map),
            pl.BlockSpec((tile, BRANCH_WIDTH), hi_map),
            full(wg), full(bg), full(wsb), full(wfx), full(wo), full(lg), full(lb),
        ],
        out_specs=pl.BlockSpec((tile, d), lambda r: (r, 0)),
        out_shape=jax.ShapeDtypeStruct((n_tok, d), f32),
        compiler_params=_params("arbitrary"),
        name="post",
    )(x2, sh, sc, g1, asb, afx_lo, afx_hi, wg, bg, wsb, wfx, wo, lg, lb)


def _ffn_kernel(x_ref, sh_ref, sc_ref, g2_ref, wg_ref, wu_ref, wd_ref, lg_ref, lb_ref, o_ref, acc_ref,
                *, alpha):
    x = x_ref[...]
    u = (_layer_norm(x) * (1.0 + sc_ref[0]) + sh_ref[0]).astype(bf16)
    d_ff = wg_ref.shape[1]
    for c0 in range(0, d_ff, FFN_CHUNK):
        gate = _dot(u, wg_ref[:, c0:c0 + FFN_CHUNK])
        up = _dot(u, wu_ref[:, c0:c0 + FFN_CHUNK])
        h = (gate * _sigmoid(gate) * up).astype(bf16)
        part = _dot(h, wd_ref[c0:c0 + FFN_CHUNK, :])
        if c0 == 0:
            acc_ref[...] = part
        else:
            acc_ref[...] += part
    r = alpha * x + g2_ref[0] * acc_ref[...]
    o_ref[...] = _layer_norm(r) * lg_ref[...] + lb_ref[...]


def _ffn(x2, sh, sc, g2, wg, wu, wd, lg, lb, seq, alpha):
    n_tok, d = x2.shape
    tile = ROW_TILE
    tiles_per_seq = seq // tile
    bmap = lambda r: (r // tiles_per_seq, 0, 0)
    resident = lambda a: pl.BlockSpec(a.shape, lambda r: (0,) * a.ndim, pipeline_mode=pl.Buffered(1))
    return pl.pallas_call(
        functools.partial(_ffn_kernel, alpha=alpha),
        grid=(n_tok // tile,),
        in_specs=[
            pl.BlockSpec((tile, d), lambda r: (r, 0)),
            pl.BlockSpec((1, 1, d), bmap),
            pl.BlockSpec((1, 1, d), bmap),
            pl.BlockSpec((1, 1, d), bmap),
            resident(wg), resident(wu), resident(wd), resident(lg), resident(lb),
        ],
        out_specs=pl.BlockSpec((tile, d), lambda r: (r, 0)),
        out_shape=jax.ShapeDtypeStruct((n_tok, d), f32),
        scratch_shapes=[pltpu.VMEM((tile, d), f32)],
        compiler_params=_params("arbitrary"),
        name="ffn",
    )(x2, sh, sc, g2, wg, wu, wd, lg, lb)


def _suffix_sum_pair(tile):
    r = jnp.arange(tile)[:, None]
    s = jnp.arange(tile)[None, :]
    u = (r >= s).astype(bf16)
    return jnp.concatenate([u, u], axis=0)


def _inclusive_upper(chunk):
    r = jnp.arange(chunk)[:, None]
    s = jnp.arange(chunk)[None, :]
    return (r <= s).astype(bf16)


def kernel(x, c, w_ada, b_ada, w_in, b_gate, b_forget, w_sb_out, w_fox_out, w_o, ln1_g, ln1_b,
           w_ffn_gate, w_ffn_up, w_ffn_down, ln2_g, ln2_b):
    batch, seq, d = x.shape
    depth = w_ada.shape[0]
    alpha = (2 * depth) ** 0.25
    n_tok = batch * seq
    qkv_cols = 6 * BRANCH_WIDTH
    off_fgate = qkv_cols
    off_bgate = off_fgate + N_HEADS
    assert w_in.shape[2] == off_bgate + 2 * d
    assert seq % ATTN_TILE == 0 and seq % FOX_TILE == 0 and seq % ROW_TILE == 0

    c_pad = jnp.zeros((8, d), f32).at[:batch].set(c)
    uu = _suffix_sum_pair(ATTN_TILE)
    tri = _inclusive_upper(256)
    col_scale = jnp.ones((qkv_cols,), f32)
    col_scale = col_scale.at[0:BRANCH_WIDTH].set(HEAD_DIM ** -0.5)
    col_scale = col_scale.at[3 * BRANCH_WIDTH:4 * BRANCH_WIDTH].set(HEAD_DIM ** -0.5)

    x2 = x.reshape(n_tok, d)
    for l in range(depth):
        ada = _ada(c_pad, w_ada[l], b_ada[l][None, :])[:batch]
        sh1, sc1, g1, sh2, sc2, g2 = [t[:, None, :] for t in jnp.split(ada, 6, axis=-1)]

        w_in_t = jnp.swapaxes(w_in[l], 0, 1)
        wqkv = (w_in_t[:qkv_cols] * col_scale[:, None]).astype(bf16)
        wf = jnp.zeros((LANES, d), f32).at[:N_HEADS].set(w_in_t[off_fgate:off_bgate]).astype(bf16)
        bf_pad = jnp.zeros((1, LANES), f32).at[0, :N_HEADS].set(b_forget[l])

        qkv, flog = _inproj(x2, sh1, sc1, wqkv, wf, seq)
        qkv = qkv.reshape(batch, seq, qkv_cols)
        fc = _fcum(flog.reshape(batch, seq, LANES), bf_pad, tri)
        fc = fc.reshape(batch, N_HEADS * (seq // FOX_TILE), FOX_TILE)

        a_sb = _sb_attention(qkv, uu, ATTN_TILE, SB_LANE_BLOCKS).reshape(n_tok, BRANCH_WIDTH)
        a_fx_lo, a_fx_hi = [a.reshape(n_tok // 2, BRANCH_WIDTH) for a in _fox_attention(qkv, fc, FOX_TILE)]

        x2 = _post(x2, sh1, sc1, g1, a_sb, a_fx_lo, a_fx_hi,
                   w_in_t[off_bgate:].astype(bf16), b_gate[l][None, :],
                   w_sb_out[l].astype(bf16), w_fox_out[l].astype(bf16), w_o[l].astype(bf16),
                   ln1_g[l][None, :], ln1_b[l][None, :], seq, alpha)
        x2 = _ffn(x2, sh2, sc2, g2,
                  w_ffn_gate[l].astype(bf16), w_ffn_up[l].astype(bf16), w_ffn_down[l].astype(bf16),
                  ln2_g[l][None, :], ln2_b[l][None, :], seq, alpha)
    return x2.reshape(batch, seq, d)
```

```python
import functools

import jax
import jax.numpy as jnp
from jax import lax
from jax.experimental import pallas as pl
from jax.experimental.pallas import tpu as pltpu

HEAD_DIM = 64
N_HEADS = 8
BRANCH_WIDTH = N_HEADS * HEAD_DIM
LANES = 128
HEADS_PER_BLOCK = LANES // HEAD_DIM
LN_EPS = 1e-5
NEG_BIG = -1e30
LOG2E = 1.4426950408889634
VMEM_LIMIT = 56 * 1024 * 1024
SB_UNDERFLOW_LOG = -105.0

ATTN_TILE = 256
SB_LANE_BLOCKS = 2
FOX_TILE = 512
ROW_TILE = 512
FFN_CHUNK = 256
ADA_COLS = 1024

f32 = jnp.float32
bf16 = jnp.bfloat16


def _dot(a, b):
    return jnp.dot(a, b, preferred_element_type=f32)


def _dot_nt(a, b):
    return lax.dot_general(a, b, (((1,), (1,)), ((), ())), preferred_element_type=f32)


def _layer_norm(x):
    mu = jnp.mean(x, axis=-1, keepdims=True)
    xc = x - mu
    var = jnp.mean(xc * xc, axis=-1, keepdims=True)
    return xc * lax.rsqrt(var + LN_EPS)


def _sigmoid(x):
    return 1.0 / (1.0 + jnp.exp(-x))


def _params(*sem):
    return pltpu.CompilerParams(dimension_semantics=sem, vmem_limit_bytes=VMEM_LIMIT)


def _ada_kernel(c_ref, w_ref, b_ref, o_ref):
    c = c_ref[...]
    c_act = (c * _sigmoid(c)).astype(bf16)
    o_ref[...] = _dot(c_act, w_ref[...].astype(bf16)) + b_ref[...]


def _ada(c_pad, w, b):
    rows, d = c_pad.shape
    n = w.shape[1]
    return pl.pallas_call(
        _ada_kernel,
        grid=(n // ADA_COLS,),
        in_specs=[
            pl.BlockSpec((rows, d), lambda j: (0, 0)),
            pl.BlockSpec((d, ADA_COLS), lambda j: (0, j)),
            pl.BlockSpec((1, ADA_COLS), lambda j: (0, j)),
        ],
        out_specs=pl.BlockSpec((rows, ADA_COLS), lambda j: (0, j)),
        out_shape=jax.ShapeDtypeStruct((rows, n), f32),
        compiler_params=_params("arbitrary"),
        name="ada",
    )(c_pad, w, b)


def _inproj_kernel(x_ref, sh_ref, sc_ref, wqkv_ref, wf_ref, qkv_ref, fl_ref, *, col_chunk):
    u = (_layer_norm(x_ref[...]) * (1.0 + sc_ref[0]) + sh_ref[0]).astype(bf16)
    n = wqkv_ref.shape[0]
    for c0 in range(0, n, col_chunk):
        qkv_ref[:, c0:c0 + col_chunk] = _dot_nt(u, wqkv_ref[c0:c0 + col_chunk, :]).astype(bf16)
    fl_ref[...] = _dot_nt(u, wf_ref[...])


def _inproj(x2, sh, sc, wqkv, wf, seq):
    n_tok, d = x2.shape
    tiles_per_seq = seq // ROW_TILE
    bmap = lambda r: (r // tiles_per_seq, 0, 0)
    return pl.pallas_call(
        functools.partial(_inproj_kernel, col_chunk=512),
        grid=(n_tok // ROW_TILE,),
        in_specs=[
            pl.BlockSpec((ROW_TILE, d), lambda r: (r, 0)),
            pl.BlockSpec((1, 1, d), bmap),
            pl.BlockSpec((1, 1, d), bmap),
            pl.BlockSpec(wqkv.shape, lambda r: (0, 0)),
            pl.BlockSpec(wf.shape, lambda r: (0, 0)),
        ],
        out_specs=[
            pl.BlockSpec((ROW_TILE, wqkv.shape[0]), lambda r: (r, 0)),
            pl.BlockSpec((ROW_TILE, LANES), lambda r: (r, 0)),
        ],
        out_shape=[
            jax.ShapeDtypeStruct((n_tok, wqkv.shape[0]), bf16),
            jax.ShapeDtypeStruct((n_tok, LANES), f32),
        ],
        compiler_params=_params("arbitrary"),
        name="inproj",
    )(x2, sh, sc, wqkv, wf)


def _split3(x):
    hi = x.astype(bf16)
    r1 = x - hi.astype(f32)
    mid = r1.astype(bf16)
    lo = (r1 - mid.astype(f32)).astype(bf16)
    return hi, mid, lo


def _fcum_kernel(fl_ref, bf_ref, tri_ref, o_ref, *, chunk):
    seq = fl_ref.shape[1]
    tri = tri_ref[...]
    carry = jnp.zeros((N_HEADS, 1), f32)
    for c0 in range(0, seq, chunk):
        logit = fl_ref[0, c0:c0 + chunk, :] + bf_ref[...]
        ls = jnp.minimum(logit, 0.0) - jnp.log1p(jnp.exp(-jnp.abs(logit)))
        ls_t = ls.T[0:N_HEADS, :]
        hi, mid, lo = _split3(ls_t)
        cs = _dot(hi, tri) + _dot(mid, tri) + _dot(lo, tri) + carry
        o_ref[0, :, c0:c0 + chunk] = cs
        carry = cs[:, chunk - 1:chunk]


def _fcum(fl3, bf_pad, tri):
    b, seq, _ = fl3.shape
    chunk = tri.shape[0]
    return pl.pallas_call(
        functools.partial(_fcum_kernel, chunk=chunk),
        grid=(b,),
        in_specs=[
            pl.BlockSpec((1, seq, LANES), lambda i: (i, 0, 0)),
            pl.BlockSpec((1, LANES), lambda i: (0, 0)),
            pl.BlockSpec(tri.shape, lambda i: (0, 0)),
        ],
        out_specs=pl.BlockSpec((1, N_HEADS, seq), lambda i: (i, 0, 0)),
        out_shape=jax.ShapeDtypeStruct((b, N_HEADS, seq), f32),
        compiler_params=_params("arbitrary"),
        name="fcum",
    )(fl3, bf_pad, tri)


def _head_queries(q):
    lane = lax.broadcasted_iota(jnp.int32, q.shape, 1)
    zero = jnp.zeros_like(q)
    return [jnp.where(lane < HEAD_DIM, q, zero), jnp.where(lane >= HEAD_DIM, q, zero)]


def _lane_tile(x, width):
    reps = width // LANES
    return x if reps == 1 else jnp.concatenate([x] * reps, axis=1)


def _merge_heads(o0, o1):
    lane = lax.broadcasted_iota(jnp.int32, o0.shape, 1)
    return jnp.where(lane < HEAD_DIM, o0, o1)


def _sb_suffix(u, mask, uu):
    log_1m = jnp.minimum(u, 0.0) - jnp.log(1.0 + jnp.exp(-jnp.abs(u)))
    if mask is not None:
        log_1m = jnp.where(mask, log_1m, 0.0)
    return _dot(log_1m.astype(bf16), uu)


def _sb_total(incl):
    return jnp.broadcast_to(incl[:, 0:1], (incl.shape[0], LANES))


def _sb_weights(u, incl, carry, mask):
    g = incl if carry is None else incl + _lane_tile(carry, u.shape[1])
    a = jnp.exp(g - u)
    if mask is not None:
        a = jnp.where(mask, a, 0.0)
    return a.astype(bf16)


def _sb_block(u, carry, mask, uu):
    incl = _sb_suffix(u, mask, uu)
    return _sb_weights(u, incl, carry, mask), _sb_total(incl)


def _sb_kernel(q_ref, k_ref, v_ref, uu_ref, o_ref, acc_ref, car_ref, *, tile, lane_blocks):
    i = pl.program_id(2)
    n_heads = lane_blocks * HEADS_PER_BLOCK
    heads = range(n_heads)
    uu = uu_ref[...]
    row = lax.broadcasted_iota(jnp.int32, (tile, tile), 0)
    col = lax.broadcasted_iota(jnp.int32, (tile, tile), 1)
    below = col < row

    def lanes_of(h):
        lb = h // HEADS_PER_BLOCK
        return slice(lb * LANES, (lb + 1) * LANES)

    qh = []
    for lb in range(lane_blocks):
        qh += _head_queries(-q_ref[0, :, lb * LANES:(lb + 1) * LANES])

    @pl.when(i == 0)
    def _():
        for h in heads:
            a, total = _sb_block(_dot_nt(qh[h], k_ref[0, 0:tile, lanes_of(h)]), None, below, uu)
            acc_ref[h] = _dot(a, v_ref[0, 0:tile, lanes_of(h)])
            car_ref[h] = total

    @pl.when(i > 0)
    def _():
        rows = pl.ds(pl.multiple_of((i - 1) * tile, tile), 2 * tile)
        u = [_dot_nt(qh[h], k_ref[0, rows, lanes_of(h)]) for h in heads]
        s_diag = [_sb_suffix(u[h][:, tile:], below, uu) for h in heads]
        s_prev = [_sb_suffix(u[h][:, :tile], None, uu) for h in heads]
        for h in heads:
            t_diag = _sb_total(s_diag[h])
            a_diag = _sb_weights(u[h][:, tile:], s_diag[h], None, below)
            a_prev = _sb_weights(u[h][:, :tile], s_prev[h], t_diag, None)
            acc_ref[h] = _dot(jnp.concatenate([a_prev, a_diag], axis=1), v_ref[0, rows, lanes_of(h)])
            car_ref[h] = t_diag + _sb_total(s_prev[h])

    def exhausted():
        worst = functools.reduce(jnp.maximum, [car_ref[h] for h in heads])
        return jnp.max(worst) < SB_UNDERFLOW_LOG

    def cond(state):
        j, done = state
        return jnp.logical_and(j >= 0, jnp.logical_not(done))

    def body(state):
        j, _ = state
        rows = pl.ds(pl.multiple_of(j * tile, tile), tile)
        for h in heads:
            a, total = _sb_block(_dot_nt(qh[h], k_ref[0, rows, lanes_of(h)]), car_ref[h], None, uu)
            acc_ref[h] += _dot(a, v_ref[0, rows, lanes_of(h)])
            car_ref[h] += total
        return j - 1, exhausted()

    lax.while_loop(cond, body, (i - 2, exhausted()))
    for lb in range(lane_blocks):
        h0 = lb * HEADS_PER_BLOCK
        o_ref[0, :, lb * LANES:(lb + 1) * LANES] = _merge_heads(acc_ref[h0], acc_ref[h0 + 1]).astype(o_ref.dtype)


def _sb_attention(qkv, uu, tile, lane_blocks):
    b, seq, _ = qkv.shape
    width = lane_blocks * LANES
    blocks = BRANCH_WIDTH // width
    n_heads = lane_blocks * HEADS_PER_BLOCK
    return pl.pallas_call(
        functools.partial(_sb_kernel, tile=tile, lane_blocks=lane_blocks),
        grid=(b, blocks, seq // tile),
        in_specs=[
            pl.BlockSpec((1, tile, width), lambda bi, p, i: (bi, i, p)),
            pl.BlockSpec((1, seq, width), lambda bi, p, i: (bi, 0, blocks + p)),
            pl.BlockSpec((1, seq, width), lambda bi, p, i: (bi, 0, 2 * blocks + p)),
            pl.BlockSpec(uu.shape, lambda bi, p, i: (0, 0)),
        ],
        out_specs=pl.BlockSpec((1, tile, width), lambda bi, p, i: (bi, i, p)),
        out_shape=jax.ShapeDtypeStruct((b, seq, BRANCH_WIDTH), bf16),
        scratch_shapes=[
            pltpu.VMEM((n_heads, tile, LANES), f32),
            pltpu.VMEM((n_heads, tile, LANES), f32),
        ],
        compiler_params=_params("parallel", "parallel", "arbitrary"),
        name="sb_attn",
    )(qkv, qkv, qkv, uu)


def _lane_fold(x, op):
    return functools.reduce(op, [x[:, t * LANES:(t + 1) * LANES] for t in range(x.shape[1] // LANES)])


def _fox_kernel(qa_ref, qb_ref, k_ref, v_ref, fc_ref, oa_ref, ob_ref, qh_ref, s_ref, m_ref, acc_ref,
                *, tile, n_blocks):
    p = pl.program_id(1)
    t = pl.program_id(2)
    n_slots = n_blocks + 1
    row = lax.broadcasted_iota(jnp.int32, (tile, tile), 0)
    col = lax.broadcasted_iota(jnp.int32, (tile, tile), 1)
    causal = col <= row

    for w, q_ref in enumerate((qa_ref, qb_ref)):
        for h, q_head in enumerate(_head_queries(q_ref[0])):
            qh_ref[w, h] = q_head
    m_ref[...] = jnp.full_like(m_ref, NEG_BIG)
    acc_ref[...] = jnp.zeros_like(acc_ref)

    def owner(c):
        if c == 0:
            return 0, t
        if c == n_slots - 1:
            return 1, n_blocks - 1 - t
        is_b = c > t
        return is_b.astype(jnp.int32), jnp.where(is_b, c - t - 1, t - c)

    for c in range(n_slots):
        w, kc = owner(c)
        k = k_ref[0, pl.ds(pl.multiple_of(kc * tile, tile), tile), :]
        for h in range(HEADS_PER_BLOCK):
            head = p * HEADS_PER_BLOCK + h
            f_key = fc_ref[0, pl.ds(head * n_blocks + kc, 1), :]
            s = _dot_nt(qh_ref[w, h], k) * LOG2E - f_key * LOG2E
            if c in (0, n_slots - 1):
                s = jnp.where(causal, s, NEG_BIG)
            m_ref[w, h] = jnp.maximum(m_ref[w, h], _lane_fold(s, jnp.maximum))
            s_ref[h, c] = s

    for w in range(2):
        for h in range(HEADS_PER_BLOCK):
            m_ref[w, h] = jnp.broadcast_to(jnp.max(m_ref[w, h], axis=1, keepdims=True), (tile, LANES))

    ones = jnp.ones((tile, LANES), bf16)
    for c in range(n_slots):
        w, kc = owner(c)
        v = jnp.concatenate([v_ref[0, pl.ds(pl.multiple_of(kc * tile, tile), tile), :], ones], axis=1)
        for h in range(HEADS_PER_BLOCK):
            pr = jnp.exp2(s_ref[h, c] - _lane_tile(m_ref[w, h], tile))
            acc_ref[w, h] += _dot(pr.astype(bf16), v)

    for w, o_ref in enumerate((oa_ref, ob_ref)):
        out = [acc_ref[w, h, :, 0:LANES] / acc_ref[w, h, :, LANES:2 * LANES] for h in range(HEADS_PER_BLOCK)]
        o_ref[0] = _merge_heads(out[0], out[1]).astype(o_ref.dtype)


def _fox_attention(qkv, fc, tile):
    b, seq, _ = qkv.shape
    blocks = BRANCH_WIDTH // LANES
    n_blocks = seq // tile
    half = n_blocks // 2
    base = 3 * blocks
    out = jax.ShapeDtypeStruct((b, seq // 2, BRANCH_WIDTH), bf16)
    return pl.pallas_call(
        functools.partial(_fox_kernel, tile=tile, n_blocks=n_blocks),
        grid=(b, blocks, half),
        in_specs=[
            pl.BlockSpec((1, tile, LANES), lambda bi, p, t: (bi, t, base + p)),
            pl.BlockSpec((1, tile, LANES), lambda bi, p, t: (bi, n_blocks - 1 - t, base + p)),
            pl.BlockSpec((1, seq, LANES), lambda bi, p, t: (bi, 0, base + blocks + p)),
            pl.BlockSpec((1, seq, LANES), lambda bi, p, t: (bi, 0, base + 2 * blocks + p)),
            pl.BlockSpec((1, N_HEADS * n_blocks, tile), lambda bi, p, t: (bi, 0, 0)),
        ],
        out_specs=[
            pl.BlockSpec((1, tile, LANES), lambda bi, p, t: (bi, t, p)),
            pl.BlockSpec((1, tile, LANES), lambda bi, p, t: (bi, half - 1 - t, p)),
        ],
        out_shape=[out, out],
        scratch_shapes=[
            pltpu.VMEM((2, HEADS_PER_BLOCK, tile, LANES), bf16),
            pltpu.VMEM((HEADS_PER_BLOCK, n_blocks + 1, tile, tile), f32),
            pltpu.VMEM((2, HEADS_PER_BLOCK, tile, LANES), f32),
            pltpu.VMEM((2, HEADS_PER_BLOCK, tile, 2 * LANES), f32),
        ],
        compiler_params=_params("parallel", "parallel", "arbitrary"),
        name="fox_attn",
    )(qkv, qkv, qkv, qkv, fc)


def _post_kernel(x_ref, sh_ref, sc_ref, g1_ref, asb_ref, afx_lo_ref, afx_hi_ref, wg_ref, bg_ref, wsb_ref,
                 wfx_ref, wo_ref, lg_ref, lb_ref, o_ref, *, alpha, tiles_per_seq):
    x = x_ref[...]
    d = x.shape[1]
    u = (_layer_norm(x) * (1.0 + sc_ref[0]) + sh_ref[0]).astype(bf16)
    y_sb = _dot(asb_ref[...], wsb_ref[...])
    in_first_half = (pl.program_id(0) % tiles_per_seq) < tiles_per_seq // 2
    a_fx = jnp.where(in_first_half, afx_lo_ref[...], afx_hi_ref[...])
    y_fx = _dot(a_fx, wfx_ref[...])
    g_sb = _sigmoid(_dot_nt(u, wg_ref[0:d, :]) + bg_ref[:, 0:d])
    g_fx = _sigmoid(_dot_nt(u, wg_ref[d:2 * d, :]) + bg_ref[:, d:2 * d])
    mixed = (g_sb * y_sb + g_fx * y_fx).astype(bf16)
    mix = _dot(mixed, wo_ref[...])
    r = alpha * x + g1_ref[0] * mix
    o_ref[...] = _layer_norm(r) * lg_ref[...] + lb_ref[...]


def _post(x2, sh, sc, g1, asb, afx_lo, afx_hi, wg, bg, wsb, wfx, wo, lg, lb, seq, alpha):
    n_tok, d = x2.shape
    tile = ROW_TILE
    tiles_per_seq = seq // tile
    half = tiles_per_seq // 2
    bmap = lambda r: (r // tiles_per_seq, 0, 0)
    lo_map = lambda r: ((r // tiles_per_seq) * half + jnp.minimum(r % tiles_per_seq, half - 1), 0)
    hi_map = lambda r: ((r // tiles_per_seq) * half + jnp.maximum(r % tiles_per_seq - half, 0), 0)
    full = lambda a: pl.BlockSpec(a.shape, lambda r: (0,) * a.ndim)
    return pl.pallas_call(
        functools.partial(_post_kernel, alpha=alpha, tiles_per_seq=tiles_per_seq),
        grid=(n_tok // tile,),
        in_specs=[
            pl.BlockSpec((tile, d), lambda r: (r, 0)),
            pl.BlockSpec((1, 1, d), bmap),
            pl.BlockSpec((1, 1, d), bmap),
            pl.BlockSpec((1, 1, d), bmap),
            pl.BlockSpec((tile, BRANCH_WIDTH), lambda r: (r, 0)),
            pl.BlockSpec((tile, BRANCH_WIDTH), lo_map),
            pl.BlockSpec((tile, BRANCH_WIDTH), hi_map),
            full(wg), full(bg), full(wsb), full(wfx), full(wo), full(lg), full(lb),
        ],
        out_specs=pl.BlockSpec((tile, d), lambda r: (r, 0)),
        out_shape=jax.ShapeDtypeStruct((n_tok, d), f32),
        compiler_params=_params("arbitrary"),
        name="post",
    )(x2, sh, sc, g1, asb, afx_lo, afx_hi, wg, bg, wsb, wfx, wo, lg, lb)


def _ffn_kernel(x_ref, sh_ref, sc_ref, g2_ref, wg_ref, wu_ref, wd_ref, lg_ref, lb_ref, o_ref, acc_ref,
                *, alpha):
    x = x_ref[...]
    u = (_layer_norm(x) * (1.0 + sc_ref[0]) + sh_ref[0]).astype(bf16)
    d_ff = wg_ref.shape[1]
    for c0 in range(0, d_ff, FFN_CHUNK):
        gate = _dot(u, wg_ref[:, c0:c0 + FFN_CHUNK])
        up = _dot(u, wu_ref[:, c0:c0 + FFN_CHUNK])
        h = (gate * _sigmoid(gate) * up).astype(bf16)
        part = _dot(h, wd_ref[c0:c0 + FFN_CHUNK, :])
        if c0 == 0:
            acc_ref[...] = part
        else:
            acc_ref[...] += part
    r = alpha * x + g2_ref[0] * acc_ref[...]
    o_ref[...] = _layer_norm(r) * lg_ref[...] + lb_ref[...]


def _ffn(x2, sh, sc, g2, wg, wu, wd, lg, lb, seq, alpha):
    n_tok, d = x2.shape
    tile = ROW_TILE
    tiles_per_seq = seq // tile
    bmap = lambda r: (r // tiles_per_seq, 0, 0)
    resident =lambda a: pl.BlockSpec(a.shape, lambda r: (0,) * a.ndim, pipeline_mode=pl.Buffered(1))
    return pl.pallas_call(
        functools.partial(_ffn_kernel, alpha=alpha),
        grid=(n_tok // tile,),
        in_specs=[
            pl.BlockSpec((tile, d), lambda r: (r, 0)),
            pl.BlockSpec((1, 1, d), bmap),
            pl.BlockSpec((1, 1, d), bmap),
            pl.BlockSpec((1, 1, d), bmap),
            resident(wg), resident(wu), resident(wd), resident(lg), resident(lb),
        ],
        out_specs=pl.BlockSpec((tile, d), lambda r: (r, 0)),
        out_shape=jax.ShapeDtypeStruct((n_tok, d), f32),
        scratch_shapes=[pltpu.VMEM((tile, d), f32)],
        compiler_params=_params("arbitrary"),
        name="ffn",
    )(x2, sh, sc, g2, wg, wu, wd, lg, lb)


def _suffix_sum_matrix(tile):
    r = jnp.arange(tile)[:, None]
    s = jnp.arange(tile)[None, :]
    return (r >= s).astype(bf16)


def _inclusive_upper(chunk):
    r = jnp.arange(chunk)[:, None]
    s = jnp.arange(chunk)[None, :]
    return (r <= s).astype(bf16)


def kernel(x, c, w_ada, b_ada, w_in, b_gate, b_forget, w_sb_out, w_fox_out, w_o, ln1_g, ln1_b,
           w_ffn_gate, w_ffn_up, w_ffn_down, ln2_g, ln2_b):
    batch, seq, d = x.shape
    depth = w_ada.shape[0]
    alpha = (2 * depth) ** 0.25
    n_tok = batch * seq
    qkv_cols = 6 * BRANCH_WIDTH
    off_fgate = qkv_cols
    off_bgate = off_fgate + N_HEADS
    assert w_in.shape[2] == off_bgate + 2 * d
    assert seq % ATTN_TILE == 0 and seq % FOX_TILE == 0 and seq % ROW_TILE == 0

    c_pad = jnp.zeros((8, d), f32).at[:batch].set(c)
    uu = _suffix_sum_matrix(ATTN_TILE)
    tri = _inclusive_upper(256)
    col_scale = jnp.ones((qkv_cols,), f32)
    col_scale = col_scale.at[0:BRANCH_WIDTH].set(HEAD_DIM ** -0.5)
    col_scale = col_scale.at[3 * BRANCH_WIDTH:4 * BRANCH_WIDTH].set(HEAD_DIM ** -0.5)

    x2 = x.reshape(n_tok, d)
    for l in range(depth):
        ada = _ada(c_pad, w_ada[l], b_ada[l][None, :])[:batch]
        sh1, sc1, g1, sh2, sc2, g2 = [t[:, None, :] for t in jnp.split(ada, 6, axis=-1)]

        w_in_t = jnp.swapaxes(w_in[l], 0, 1)
        wqkv = (w_in_t[:qkv_cols] * col_scale[:, None]).astype(bf16)
        wf = jnp.zeros((LANES, d), f32).at[:N_HEADS].set(w_in_t[off_fgate:off_bgate]).astype(bf16)
        bf_pad = jnp.zeros((1, LANES), f32).at[0, :N_HEADS].set(b_forget[l])

        qkv, flog = _inproj(x2, sh1, sc1, wqkv, wf, seq)
        qkv = qkv.reshape(batch, seq, qkv_cols)
        fc = _fcum(flog.reshape(batch, seq, LANES), bf_pad, tri)
        fc = fc.reshape(batch, N_HEADS * (seq // FOX_TILE), FOX_TILE)

        a_sb = _sb_attention(qkv, uu, ATTN_TILE, SB_LANE_BLOCKS).reshape(n_tok, BRANCH_WIDTH)
        a_fx_lo, a_fx_hi = [a.reshape(n_tok // 2, BRANCH_WIDTH) for a in _fox_attention(qkv, fc, FOX_TILE)]

        x2 = _post(x2, sh1, sc1, g1, a_sb, a_fx_lo, a_fx_hi,
                   w_in_t[off_bgate:].astype(bf16), b_gate[l][None, :],
                   w_sb_out[l].astype(bf16), w_fox_out[l].astype(bf16), w_o[l].astype(bf16),
                   ln1_g[l][None, :], ln1_b[l][None, :], seq, alpha)
        x2 = _ffn(x2, sh2, sc2, g2,
                  w_ffn_gate[l].astype(bf16), w_ffn_up[l].astype(bf16), w_ffn_down[l].astype(bf16),
                  ln2_g[l][None, :], ln2_b[l][None, :], seq, alpha)
    return x2.reshape(batch, seq, d)
```

```python
import functools

import jax
import jax.numpy as jnp
from jax import lax
from jax.experimental import pallas as pl
from jax.experimental.pallas import tpu as pltpu

HEAD_DIM = 64
N_HEADS = 8
BRANCH_WIDTH = N_HEADS * HEAD_DIM
LANES = 128
HEADS_PER_BLOCK = LANES // HEAD_DIM
LN_EPS = 1e-5
NEG_BIG = -1e30
LOG2E = 1.4426950408889634
VMEM_LIMIT = 56 * 1024 * 1024
SB_UNDERFLOW_LOG = -105.0

ATTN_TILE = 256
SB_LANE_BLOCKS = 4
FOX_TILE = 512
ROW_TILE = 512
FFN_CHUNK = 256
ADA_COLS = 1024

f32 = jnp.float32
bf16 = jnp.bfloat16


def _dot(a, b):
    return jnp.dot(a, b, preferred_element_type=f32)


def _dot_nt(a, b):
    return lax.dot_general(a, b, (((1,), (1,)), ((), ())), preferred_element_type=f32)


def _layer_norm(x):
    mu = jnp.mean(x, axis=-1, keepdims=True)
    xc = x - mu
    var = jnp.mean(xc * xc, axis=-1, keepdims=True)
    return xc * lax.rsqrt(var + LN_EPS)


def _sigmoid(x):
    return 1.0 / (1.0 + jnp.exp(-x))


def _params(*sem):
    return pltpu.CompilerParams(dimension_semantics=sem, vmem_limit_bytes=VMEM_LIMIT)


def _ada_kernel(c_ref, w_ref, b_ref, o_ref):
    c = c_ref[...]
    c_act = (c * _sigmoid(c)).astype(bf16)
    o_ref[...] = _dot(c_act, w_ref[...].astype(bf16)) + b_ref[...]


def _ada(c_pad, w, b):
    rows, d = c_pad.shape
    n = w.shape[1]
    return pl.pallas_call(
        _ada_kernel,
        grid=(n // ADA_COLS,),
        in_specs=[
            pl.BlockSpec((rows, d), lambda j: (0, 0)),
            pl.BlockSpec((d, ADA_COLS), lambda j: (0, j)),
            pl.BlockSpec((1, ADA_COLS), lambda j: (0, j)),
        ],
        out_specs=pl.BlockSpec((rows, ADA_COLS), lambda j: (0, j)),
        out_shape=jax.ShapeDtypeStruct((rows, n), f32),
        compiler_params=_params("arbitrary"),
        name="ada",
    )(c_pad, w, b)


def _inproj_kernel(x_ref, sh_ref, sc_ref, wqkv_ref, wf_ref, qkv_ref, fl_ref, *, col_chunk):
    u = (_layer_norm(x_ref[...]) * (1.0 + sc_ref[0]) + sh_ref[0]).astype(bf16)
    n = wqkv_ref.shape[0]
    for c0 in range(0, n, col_chunk):
        qkv_ref[:, c0:c0 + col_chunk] = _dot_nt(u, wqkv_ref[c0:c0 + col_chunk, :]).astype(bf16)
    fl_ref[...] = _dot_nt(u, wf_ref[...])


def _inproj(x2, sh, sc, wqkv, wf, seq):
    n_tok, d = x2.shape
    tiles_per_seq = seq // ROW_TILE
    bmap = lambda r: (r // tiles_per_seq, 0, 0)
    return pl.pallas_call(
        functools.partial(_inproj_kernel, col_chunk=512),
        grid=(n_tok // ROW_TILE,),
        in_specs=[
            pl.BlockSpec((ROW_TILE, d), lambda r: (r, 0)),
            pl.BlockSpec((1, 1, d), bmap),
            pl.BlockSpec((1, 1, d), bmap),
            pl.BlockSpec(wqkv.shape, lambda r: (0, 0)),
            pl.BlockSpec(wf.shape, lambda r: (0, 0)),
        ],
        out_specs=[
            pl.BlockSpec((ROW_TILE, wqkv.shape[0]), lambda r: (r, 0)),
            pl.BlockSpec((ROW_TILE, LANES), lambda r: (r, 0)),
        ],
        out_shape=[
            jax.ShapeDtypeStruct((n_tok, wqkv.shape[0]), bf16),
            jax.ShapeDtypeStruct((n_tok, LANES), f32),
        ],
        compiler_params=_params("arbitrary"),
        name="inproj",
    )(x2, sh, sc, wqkv, wf)


def _split3(x):
    hi = x.astype(bf16)
    r1 = x - hi.astype(f32)
    mid = r1.astype(bf16)
    lo = (r1 - mid.astype(f32)).astype(bf16)
    return hi, mid, lo


def _fcum_kernel(fl_ref, bf_ref, tri_ref, o_ref, *, chunk):
    seq = fl_ref.shape[1]
    tri = tri_ref[...]
    carry = jnp.zeros((N_HEADS, 1), f32)
    for c0 in range(0, seq, chunk):
        logit = fl_ref[0, c0:c0 + chunk, :] + bf_ref[...]
        ls = jnp.minimum(logit, 0.0) - jnp.log1p(jnp.exp(-jnp.abs(logit)))
        ls_t = ls.T[0:N_HEADS, :]
        hi, mid, lo = _split3(ls_t)
        cs = _dot(hi, tri) + _dot(mid, tri) + _dot(lo, tri) + carry
        o_ref[0, :, c0:c0 + chunk] = cs
        carry = cs[:, chunk - 1:chunk]


def _fcum(fl3, bf_pad, tri):
    b, seq, _ = fl3.shape
    chunk = tri.shape[0]
    return pl.pallas_call(
        functools.partial(_fcum_kernel, chunk=chunk),
        grid=(b,),
        in_specs=[
            pl.BlockSpec((1, seq, LANES), lambda i: (i, 0, 0)),
            pl.BlockSpec((1, LANES), lambda i: (0, 0)),
            pl.BlockSpec(tri.shape, lambda i: (0, 0)),
        ],
        out_specs=pl.BlockSpec((1, N_HEADS, seq), lambda i: (i, 0, 0)),
        out_shape=jax.ShapeDtypeStruct((b, N_HEADS, seq), f32),
        compiler_params=_params("arbitrary"),
        name="fcum",
    )(fl3, bf_pad, tri)


def _head_queries(q):
    lane = lax.broadcasted_iota(jnp.int32, q.shape, 1)
    zero = jnp.zeros_like(q)
    return [jnp.where(lane < HEAD_DIM, q, zero), jnp.where(lane >= HEAD_DIM, q, zero)]


def _lane_tile(x, width):
    reps = width // LANES
    return x if reps == 1 else jnp.concatenate([x] * reps, axis=1)


def _merge_heads(o0, o1):
    lane = lax.broadcasted_iota(jnp.int32, o0.shape, 1)
    return jnp.where(lane < HEAD_DIM, o0, o1)


def _sb_suffix(u, mask, uu):
    log_1m = jnp.minimum(u, 0.0) - jnp.log(1.0 + jnp.exp(-jnp.abs(u)))
    if mask is not None:
        log_1m = jnp.where(mask, log_1m, 0.0)
    return _dot(log_1m.astype(bf16), uu)


def _sb_total(incl):
    return jnp.broadcast_to(incl[:, 0:1], (incl.shape[0], LANES))


def _sb_weights(u, incl, carry, mask):
    g = incl if carry is None else incl + _lane_tile(carry, u.shape[1])
    a = jnp.exp(g - u)
    if mask is not None:
        a = jnp.where(mask, a, 0.0)
    return a.astype(bf16)


def _sb_block(u, carry, mask, uu):
    incl = _sb_suffix(u, mask, uu)
    return _sb_weights(u, incl, carry, mask), _sb_total(incl)


def _sb_kernel(q_ref, k_ref, v_ref, uu_ref, o_ref, acc_ref, car_ref, *, tile, lane_blocks):
    i = pl.program_id(2)
    n_heads = lane_blocks * HEADS_PER_BLOCK
    heads = range(n_heads)
    uu = uu_ref[...]
    row = lax.broadcasted_iota(jnp.int32, (tile, tile), 0)
    col = lax.broadcasted_iota(jnp.int32, (tile, tile), 1)
    below = col < row

    def lanes_of(h):
        lb = h // HEADS_PER_BLOCK
        return slice(lb * LANES, (lb + 1) * LANES)

    qh = []
    for lb in range(lane_blocks):
        qh += _head_queries(-q_ref[0, :, lb * LANES:(lb + 1) * LANES])

    @pl.when(i == 0)
    def _():
        for h in heads:
            a, total = _sb_block(_dot_nt(qh[h], k_ref[0, 0:tile, lanes_of(h)]), None, below, uu)
            acc_ref[h] = _dot(a, v_ref[0, 0:tile, lanes_of(h)])
            car_ref[h] = total

    @pl.when(i > 0)
    def _():
        rows = pl.ds(pl.multiple_of((i - 1) * tile, tile), 2 * tile)
        u = [_dot_nt(qh[h], k_ref[0, rows, lanes_of(h)]) for h in heads]
        s_diag = [_sb_suffix(u[h][:, tile:], below, uu) for h in heads]
        s_prev = [_sb_suffix(u[h][:, :tile], None, uu) for h in heads]
        for h in heads:
            t_diag = _sb_total(s_diag[h])
            a_diag = _sb_weights(u[h][:, tile:], s_diag[h], None, below)
            a_prev = _sb_weights(u[h][:, :tile], s_prev[h], t_diag, None)
            acc_ref[h] = _dot(jnp.concatenate([a_prev, a_diag], axis=1), v_ref[0, rows, lanes_of(h)])
            car_ref[h] = t_diag + _sb_total(s_prev[h])

    def exhausted():
        worst = functools.reduce(jnp.maximum, [car_ref[h] for h in heads])
        return jnp.max(worst) < SB_UNDERFLOW_LOG

    def cond(state):
        j, done = state
        return jnp.logical_and(j >= 0, jnp.logical_not(done))

    def body(state):
        j, _ = state
        rows = pl.ds(pl.multiple_of(j * tile, tile), tile)
        for h in heads:
            a, total = _sb_block(_dot_nt(qh[h], k_ref[0, rows, lanes_of(h)]), car_ref[h], None, uu)
            acc_ref[h] += _dot(a, v_ref[0, rows, lanes_of(h)])
            car_ref[h] += total
        return j - 1, exhausted()

    lax.while_loop(cond, body, (i - 2, exhausted()))
    for lb in range(lane_blocks):
        h0 = lb * HEADS_PER_BLOCK
        o_ref[0, :, lb * LANES:(lb + 1) * LANES] = _merge_heads(acc_ref[h0], acc_ref[h0 + 1]).astype(o_ref.dtype)


def _sb_attention(qkv, uu, tile, lane_blocks):
    b, seq, _ = qkv.shape
    width = lane_blocks * LANES
    blocks = BRANCH_WIDTH // width
    n_heads = lane_blocks * HEADS_PER_BLOCK
    return pl.pallas_call(
        functools.partial(_sb_kernel, tile=tile, lane_blocks=lane_blocks),
        grid=(b, blocks, seq // tile),
        in_specs=[
            pl.BlockSpec((1, tile, width), lambda bi, p, i: (bi, i, p)),
            pl.BlockSpec((1, seq, width), lambda bi, p, i: (bi, 0, blocks + p)),
            pl.BlockSpec((1, seq, width), lambda bi, p, i: (bi, 0, 2 * blocks + p)),
            pl.BlockSpec(uu.shape, lambda bi, p, i: (0, 0)),
        ],
        out_specs=pl.BlockSpec((1, tile, width), lambda bi, p, i: (bi, i, p)),
        out_shape=jax.ShapeDtypeStruct((b, seq, BRANCH_WIDTH), bf16),
        scratch_shapes=[
            pltpu.VMEM((n_heads, tile, LANES), f32),
            pltpu.VMEM((n_heads, tile, LANES), f32),
        ],
        compiler_params=_params("parallel", "parallel", "arbitrary"),
        name="sb_attn",
    )(qkv, qkv, qkv, uu)


def _lane_fold(x, op):
    return functools.reduce(op, [x[:, t * LANES:(t + 1) * LANES] for t in range(x.shape[1] // LANES)])


def _fox_kernel(qa_ref, qb_ref, k_ref, v_ref, fc_ref, oa_ref, ob_ref, qh_ref, s_ref, m_ref, acc_ref,
                *, tile, n_blocks):
    p = pl.program_id(1)
    t = pl.program_id(2)
    n_slots = n_blocks + 1
    row = lax.broadcasted_iota(jnp.int32, (tile, tile), 0)
    col = lax.broadcasted_iota(jnp.int32, (tile, tile), 1)
    causal = col <= row

    for w, q_ref in enumerate((qa_ref, qb_ref)):
        for h, q_head in enumerate(_head_queries(q_ref[0])):
            qh_ref[w, h] = q_head
    m_ref[...] = jnp.full_like(m_ref, NEG_BIG)
    acc_ref[...] = jnp.zeros_like(acc_ref)

    def owner(c):
        if c == 0:
            return 0, t
        if c == n_slots - 1:
            return 1, n_blocks - 1 - t
        is_b = c > t
        return is_b.astype(jnp.int32), jnp.where(is_b, c - t - 1, t - c)

    def scores(c):
        w, kc = owner(c)
        k = k_ref[0, pl.ds(pl.multiple_of(kc * tile, tile), tile), :]
        for h in range(HEADS_PER_BLOCK):
            head = p * HEADS_PER_BLOCK + h
            f_key = fc_ref[0, pl.ds(head * n_blocks + kc, 1), :]
            s = _dot_nt(qh_ref[w, h], k) * LOG2E - f_key * LOG2E
            if c in (0, n_slots - 1):
                s = jnp.where(causal, s, NEG_BIG)
            m_ref[w, h] = jnp.maximum(m_ref[w, h], _lane_fold(s, jnp.maximum))
            s_ref[h, c] = s

    def finish_max(w):
        for h in range(HEADS_PER_BLOCK):
            m_ref[w, h] = jnp.broadcast_to(jnp.max(m_ref[w, h], axis=1, keepdims=True), (tile, LANES))

    for c in range(1, n_slots - 1):
        scores(c)
    scores(0)
    finish_max(0)
    scores(n_slots - 1)
    finish_max(1)

    ones = jnp.ones((tile, LANES), bf16)
    for c in list(range(1, n_slots - 1)) + [0, n_slots - 1]:
        w, kc = owner(c)
        v = jnp.concatenate([v_ref[0, pl.ds(pl.multiple_of(kc * tile, tile), tile), :], ones], axis=1)
        for h in range(HEADS_PER_BLOCK):
            pr = jnp.exp2(s_ref[h, c] - _lane_tile(m_ref[w, h], tile))
            acc_ref[w, h] += _dot(pr.astype(bf16), v)

    for w, o_ref in enumerate((oa_ref, ob_ref)):
        out = [acc_ref[w, h, :, 0:LANES] / acc_ref[w, h, :, LANES:2 * LANES] for h in range(HEADS_PER_BLOCK)]
        o_ref[0] = _merge_heads(out[0], out[1]).astype(o_ref.dtype)


def _fox_attention(qkv, fc, tile):
    b, seq, _ = qkv.shape
    blocks = BRANCH_WIDTH // LANES
    n_blocks = seq // tile
    half = n_blocks // 2
    base = 3 * blocks
    out = jax.ShapeDtypeStruct((b, seq // 2, BRANCH_WIDTH), bf16)
    return pl.pallas_call(
        functools.partial(_fox_kernel, tile=tile, n_blocks=n_blocks),
        grid=(b, blocks, half),
        in_specs=[
            pl.BlockSpec((1, tile, LANES), lambda bi, p, t: (bi, t, base + p)),
            pl.BlockSpec((1, tile, LANES), lambda bi, p, t: (bi, n_blocks - 1 - t, base + p)),
            pl.BlockSpec((1, seq, LANES), lambda bi, p, t: (bi, 0, base + blocks + p)),
            pl.BlockSpec((1, seq, LANES), lambda bi, p, t: (bi, 0, base + 2 * blocks + p)),
            pl.BlockSpec((1, N_HEADS * n_blocks, tile), lambda bi, p, t: (bi, 0, 0)),
        ],
        out_specs=[
            pl.BlockSpec((1, tile, LANES), lambda bi, p, t: (bi, t, p)),
            pl.BlockSpec((1, tile, LANES), lambda bi, p, t: (bi, half - 1 - t, p)),
        ],
        out_shape=[out, out],
        scratch_shapes=[
            pltpu.VMEM((2, HEADS_PER_BLOCK, tile, LANES), bf16),
            pltpu.VMEM((HEADS_PER_BLOCK, n_blocks + 1, tile, tile), f32),
            pltpu.VMEM((2, HEADS_PER_BLOCK, tile, LANES), f32),
            pltpu.VMEM((2, HEADS_PER_BLOCK, tile, 2 * LANES), f32),
        ],
        compiler_params=_params("parallel", "parallel", "arbitrary"),
        name="fox_attn",
    )(qkv, qkv, qkv, qkv, fc)


def _post_kernel(x_ref, sh_ref, sc_ref, g1_ref, asb_ref, afx_lo_ref, afx_hi_ref, wg_ref, bg_ref, wsb_ref,
                 wfx_ref, wo_ref, lg_ref, lb_ref, o_ref, *, alpha, tiles_per_seq):
    x = x_ref[...]
    d = x.shape[1]
    u = (_layer_norm(x) * (1.0 + sc_ref[0]) + sh_ref[0]).astype(bf16)
    y_sb = _dot(asb_ref[...], wsb_ref[...])
    in_first_half = (pl.program_id(0) % tiles_per_seq) < tiles_per_seq // 2
    a_fx = jnp.where(in_first_half, afx_lo_ref[...], afx_hi_ref[...])
    y_fx = _dot(a_fx, wfx_ref[...])
    g_sb = _sigmoid(_dot_nt(u, wg_ref[0:d, :]) + bg_ref[:, 0:d])
    g_fx = _sigmoid(_dot_nt(u, wg_ref[d:2 * d, :]) + bg_ref[:, d:2 * d])
    mixed = (g_sb * y_sb + g_fx * y_fx).astype(bf16)
    mix = _dot(mixed, wo_ref[...])
    r = alpha * x + g1_ref[0] * mix
    o_ref[...] = _layer_norm(r) * lg_ref[...] + lb_ref[...]


def _post(x2, sh, sc, g1, asb, afx_lo, afx_hi, wg, bg, wsb, wfx, wo, lg, lb, seq, alpha):
    n_tok, d = x2.shape
    tile = ROW_TILE
    tiles_per_seq = seq // tile
    half = tiles_per_seq // 2
    bmap = lambda r: (r // tiles_per_seq, 0, 0)
    lo_map = lambda r: ((r // tiles_per_seq) * half + jnp.minimum(r % tiles_per_seq, half - 1), 0)
    hi_map = lambda r: ((r // tiles_per_seq) * half + jnp.maximum(r % tiles_per_seq - half, 0), 0)
    full = lambda a: pl.BlockSpec(a.shape, lambda r: (0,) * a.ndim)
    return pl.pallas_call(
        functools.partial(_post_kernel, alpha=alpha, tiles_per_seq=tiles_per_seq),
        grid=(n_tok // tile,),
        in_specs=[
            pl.BlockSpec((tile, d), lambda r: (r, 0)),
            pl.BlockSpec((1, 1, d), bmap),
            pl.BlockSpec((1, 1, d), bmap),
            pl.BlockSpec((1, 1, d), bmap),
            pl.BlockSpec((tile, BRANCH_WIDTH), lambda r: (r, 0)),
            pl.BlockSpec((tile, BRANCH_WIDTH), lo_map),
            pl.BlockSpec((tile, BRANCH_WIDTH), hi_map),
            full(wg), full(bg), full(wsb), full(wfx), full(wo), full(lg), full(lb),
        ],
        out_specs=pl.BlockSpec((tile, d), lambda r: (r, 0)),
        out_shape=jax.ShapeDtypeStruct((n_tok, d), f32),
        compiler_params=_params("arbitrary"),
        name="post",
    )(x2, sh, sc, g1, asb, afx_lo, afx_hi, wg, bg, wsb, wfx, wo, lg, lb)


def _ffn_kernel(x_ref, sh_ref, sc_ref, g2_ref, wg_ref, wu_ref, wd_ref, lg_ref, lb_ref, o_ref, acc_ref,
                *, alpha):
    x = x_ref[...]
    u = (_layer_norm(x) * (1.0 + sc_ref[0]) + sh_ref[0]).astype(bf16)
    d_ff = wg_ref.shape[1]
    for c0 in range(0, d_ff, FFN_CHUNK):
        gate = _dot(u, wg_ref[:, c0:c0 + FFN_CHUNK])
        up = _dot(u, wu_ref[:, c0:c0 + FFN_CHUNK])
        h = (gate * _sigmoid(gate) * up).astype(bf16)
        part = _dot(h, wd_ref[c0:c0 + FFN_CHUNK, :])
        if c0 == 0:
            acc_ref[...] = part
        else:
            acc_ref[...] += part
    r = alpha * x + g2_ref[0] * acc_ref[...]
    o_ref[...] = _layer_norm(r) * lg_ref[...] + lb_ref[...]


def _ffn(x2, sh, sc, g2, wg, wu, wd, lg, lb, seq, alpha):
    n_tok, d = x2.shape
    tile = ROW_TILE
    tiles_per_seq = seq // tile
    bmap = lambda r: (r // tiles_per_seq, 0, 0)
    resident =lambda a: pl.BlockSpec(a.shape, lambda r: (0,) * a.ndim, pipeline_mode=pl.Buffered(1))
    return pl.pallas_call(
        functools.partial(_ffn_kernel, alpha=alpha),
        grid=(n_tok // tile,),
        in_specs=[
            pl.BlockSpec((tile, d), lambda r: (r, 0)),
            pl.BlockSpec((1, 1, d), bmap),
            pl.BlockSpec((1, 1, d), bmap),
            pl.BlockSpec((1, 1, d), bmap),
            resident(wg), resident(wu), resident(wd), resident(lg), resident(lb),
        ],
        out_specs=pl.BlockSpec((tile, d), lambda r: (r, 0)),
        out_shape=jax.ShapeDtypeStruct((n_tok, d), f32),
        scratch_shapes=[pltpu.VMEM((tile, d), f32)],
        compiler_params=_params("arbitrary"),
        name="ffn",
    )(x2, sh, sc, g2, wg, wu, wd, lg, lb)


def _suffix_sum_matrix(tile):
    r = jnp.arange(tile)[:, None]
    s = jnp.arange(tile)[None, :]
    return (r >= s).astype(bf16)


def _inclusive_upper(chunk):
    r = jnp.arange(chunk)[:, None]
    s = jnp.arange(chunk)[None, :]
    return (r <= s).astype(bf16)


def kernel(x, c, w_ada, b_ada, w_in, b_gate, b_forget, w_sb_out, w_fox_out, w_o, ln1_g, ln1_b,
           w_ffn_gate, w_ffn_up, w_ffn_down, ln2_g, ln2_b):
    batch, seq, d = x.shape
    depth = w_ada.shape[0]
    alpha = (2 * depth) ** 0.25
    n_tok = batch * seq
    qkv_cols = 6 * BRANCH_WIDTH
    off_fgate = qkv_cols
    off_bgate = off_fgate + N_HEADS
    assert w_in.shape[2] == off_bgate + 2 * d
    assert seq % ATTN_TILE == 0 and seq % FOX_TILE == 0 and seq % ROW_TILE == 0

    c_pad = jnp.zeros((8, d), f32).at[:batch].set(c)
    uu = _suffix_sum_matrix(ATTN_TILE)
    tri = _inclusive_upper(256)
    col_scale = jnp.ones((qkv_cols,), f32)
    col_scale = col_scale.at[0:BRANCH_WIDTH].set(HEAD_DIM ** -0.5)
    col_scale = col_scale.at[3 * BRANCH_WIDTH:4 * BRANCH_WIDTH].set(HEAD_DIM ** -0.5)

    x2 = x.reshape(n_tok, d)
    for l in range(depth):
        ada = _ada(c_pad, w_ada[l], b_ada[l][None, :])[:batch]
        sh1, sc1, g1, sh2, sc2, g2 = [t[:, None, :] for t in jnp.split(ada, 6, axis=-1)]

        w_in_t = jnp.swapaxes(w_in[l], 0, 1)
        wqkv = (w_in_t[:qkv_cols] * col_scale[:, None]).astype(bf16)
        wf = jnp.zeros((LANES, d), f32).at[:N_HEADS].set(w_in_t[off_fgate:off_bgate]).astype(bf16)
        bf_pad = jnp.zeros((1, LANES), f32).at[0, :N_HEADS].set(b_forget[l])

        qkv, flog = _inproj(x2, sh1, sc1, wqkv, wf, seq)
        qkv = qkv.reshape(batch, seq, qkv_cols)
        fc = _fcum(flog.reshape(batch, seq, LANES), bf_pad, tri)
        fc = fc.reshape(batch, N_HEADS * (seq // FOX_TILE), FOX_TILE)

        a_sb = _sb_attention(qkv, uu, ATTN_TILE, SB_LANE_BLOCKS).reshape(n_tok, BRANCH_WIDTH)
        a_fx_lo, a_fx_hi = [a.reshape(n_tok // 2, BRANCH_WIDTH) for a in _fox_attention(qkv, fc, FOX_TILE)]

        x2 = _post(x2, sh1, sc1, g1, a_sb, a_fx_lo, a_fx_hi,
                   w_in_t[off_bgate:].astype(bf16), b_gate[l][None, :],
                   w_sb_out[l].astype(bf16), w_fox_out[l].astype(bf16), w_o[l].astype(bf16),
                   ln1_g[l][None, :], ln1_b[l][None, :], seq, alpha)
        x2 = _ffn(x2, sh2, sc2, g2,
                  w_ffn_gate[l].astype(bf16), w_ffn_up[l].astype(bf16), w_ffn_down[l].astype(bf16),
                  ln2_g[l][None, :], ln2_b[l][None, :], seq, alpha)
    return x2.reshape(batch, seq, d)
```

```python
import functools

import jax
import jax.numpy as jnp
from jax import lax
from jax.experimental import pallas as pl
from jax.experimental.pallas import tpu as pltpu

HEAD_DIM = 64
N_HEADS = 8
BRANCH_WIDTH = N_HEADS * HEAD_DIM
LANES = 128
HEADS_PER_BLOCK = LANES // HEAD_DIM
LN_EPS = 1e-5
NEG_BIG = -1e30
LOG2E = 1.4426950408889634
VMEM_LIMIT = 56 * 1024 * 1024
SB_UNDERFLOW_LOG = -105.0

ATTN_TILE = 256
SB_LANE_BLOCKS = 4
FOX_TILE = 512
ROW_TILE = 512
FFN_CHUNK = 256
ADA_COLS = 1024

f32 = jnp.float32
bf16 = jnp.bfloat16


def _dot(a, b):
    return jnp.dot(a, b, preferred_element_type=f32)


def _dot_nt(a, b):
    return lax.dot_general(a, b, (((1,), (1,)), ((), ())), preferred_element_type=f32)


def _layer_norm(x):
    mu = jnp.mean(x, axis=-1, keepdims=True)
    xc = x - mu
    var = jnp.mean(xc * xc, axis=-1, keepdims=True)
    return xc * lax.rsqrt(var + LN_EPS)


def _sigmoid(x):
    return 1.0 / (1.0 + jnp.exp(-x))


def _params(*sem):
    return pltpu.CompilerParams(dimension_semantics=sem, vmem_limit_bytes=VMEM_LIMIT)


def _ada_kernel(c_ref, w_ref, b_ref, o_ref):
    c = c_ref[...]
    c_act = (c * _sigmoid(c)).astype(bf16)
    o_ref[...] = _dot(c_act, w_ref[...].astype(bf16)) + b_ref[...]


def _ada(c_pad, w, b):
    rows, d = c_pad.shape
    n = w.shape[1]
    return pl.pallas_call(
        _ada_kernel,
        grid=(n // ADA_COLS,),
        in_specs=[
            pl.BlockSpec((rows, d), lambda j: (0, 0)),
            pl.BlockSpec((d, ADA_COLS), lambda j: (0, j)),
            pl.BlockSpec((1, ADA_COLS), lambda j: (0, j)),
        ],
        out_specs=pl.BlockSpec((rows, ADA_COLS), lambda j: (0, j)),
        out_shape=jax.ShapeDtypeStruct((rows, n), f32),
        compiler_params=_params("arbitrary"),
        name="ada",
    )(c_pad, w, b)


def _inproj_kernel(x_ref, sh_ref, sc_ref, wqkv_ref, wf_ref, qkv_ref, fl_ref):
    u = (_layer_norm(x_ref[...]) * (1.0 + sc_ref[0]) + sh_ref[0]).astype(bf16)
    for g in range(wqkv_ref.shape[0] // BRANCH_WIDTH):
        cols = slice(g * BRANCH_WIDTH, (g + 1) * BRANCH_WIDTH)
        y = _dot_nt(u, wqkv_ref[cols, :].astype(bf16))
        if g % 3 == 0:
            y = y * HEAD_DIM ** -0.5
        qkv_ref[:, cols] = y.astype(bf16)
    fl_ref[...] = _dot_nt(u, wf_ref[...].astype(bf16))


def _inproj(x2, sh, sc, w_in_t, seq):
    n_tok, d = x2.shape
    qkv_cols = 6 * BRANCH_WIDTH
    tiles_per_seq = seq // ROW_TILE
    bmap = lambda r: (r // tiles_per_seq, 0, 0)
    return pl.pallas_call(
        _inproj_kernel,
        grid=(n_tok // ROW_TILE,),
        in_specs=[
            pl.BlockSpec((ROW_TILE, d), lambda r: (r, 0)),
            pl.BlockSpec((1, 1, d), bmap),
            pl.BlockSpec((1, 1, d), bmap),
            pl.BlockSpec((qkv_cols, d), lambda r: (0, 0), pipeline_mode=pl.Buffered(1)),
            pl.BlockSpec((LANES, d), lambda r: (qkv_cols // LANES, 0), pipeline_mode=pl.Buffered(1)),
        ],
        out_specs=[
            pl.BlockSpec((ROW_TILE, qkv_cols), lambda r: (r, 0)),
            pl.BlockSpec((ROW_TILE, LANES), lambda r: (r, 0)),
        ],
        out_shape=[
            jax.ShapeDtypeStruct((n_tok, qkv_cols), bf16),
            jax.ShapeDtypeStruct((n_tok, LANES), f32),
        ],
        compiler_params=_params("arbitrary"),
        name="inproj",
    )(x2, sh, sc, w_in_t, w_in_t)


def _split3(x):
    hi = x.astype(bf16)
    r1 = x - hi.astype(f32)
    mid = r1.astype(bf16)
    lo = (r1 - mid.astype(f32)).astype(bf16)
    return hi, mid, lo


def _fcum_kernel(fl_ref, bf_ref, tri_ref, o_ref, *, chunk):
    seq = fl_ref.shape[1]
    tri = tri_ref[...]
    carry = jnp.zeros((N_HEADS, 1), f32)
    for c0 in range(0, seq, chunk):
        logit = fl_ref[0, c0:c0 + chunk, :] + bf_ref[...]
        ls = jnp.minimum(logit, 0.0) - jnp.log1p(jnp.exp(-jnp.abs(logit)))
        ls_t = ls.T[0:N_HEADS, :]
        hi, mid, lo = _split3(ls_t)
        cs = _dot(hi, tri) + _dot(mid, tri) + _dot(lo, tri) + carry
        o_ref[0, :, c0:c0 + chunk] = cs
        carry = cs[:, chunk - 1:chunk]


def _fcum(fl3, bf_pad, tri):
    b, seq, _ = fl3.shape
    chunk = tri.shape[0]
    return pl.pallas_call(
        functools.partial(_fcum_kernel, chunk=chunk),
        grid=(b,),
        in_specs=[
            pl.BlockSpec((1, seq, LANES), lambda i: (i, 0, 0)),
            pl.BlockSpec((1, LANES), lambda i: (0, 0)),
            pl.BlockSpec(tri.shape, lambda i: (0, 0)),
        ],
        out_specs=pl.BlockSpec((1, N_HEADS, seq), lambda i: (i, 0, 0)),
        out_shape=jax.ShapeDtypeStruct((b, N_HEADS, seq), f32),
        compiler_params=_params("arbitrary"),
        name="fcum",
    )(fl3, bf_pad, tri)


def _head_queries(q):
    lane = lax.broadcasted_iota(jnp.int32, q.shape, 1)
    zero = jnp.zeros_like(q)
    return [jnp.where(lane < HEAD_DIM, q, zero), jnp.where(lane >= HEAD_DIM, q, zero)]


def _lane_tile(x, width):
    reps = width // LANES
    return x if reps == 1 else jnp.concatenate([x] * reps, axis=1)


def _merge_heads(o0, o1):
    lane = lax.broadcasted_iota(jnp.int32, o0.shape, 1)
    return jnp.where(lane < HEAD_DIM, o0, o1)


def _sb_suffix(u, mask, uu):
    log_1m = jnp.minimum(u, 0.0) - jnp.log(1.0 + jnp.exp(-jnp.abs(u)))
    if mask is not None:
        log_1m = jnp.where(mask, log_1m, 0.0)
    return _dot(log_1m.astype(bf16), uu)


def _sb_total(incl):
    return jnp.broadcast_to(incl[:, 0:1], (incl.shape[0], LANES))


def _sb_weights(u, incl, carry, mask):
    g = incl if carry is None else incl + _lane_tile(carry, u.shape[1])
    a = jnp.exp(g - u)
    if mask is not None:
        a = jnp.where(mask, a, 0.0)
    return a.astype(bf16)


def _sb_block(u, carry, mask, uu):
    incl = _sb_suffix(u, mask, uu)
    return _sb_weights(u, incl, carry, mask), _sb_total(incl)


def _sb_kernel(q_ref, k_ref, v_ref, uu_ref, o_ref, acc_ref, car_ref, *, tile, lane_blocks):
    i = pl.program_id(2)
    n_heads = lane_blocks * HEADS_PER_BLOCK
    heads = range(n_heads)
    uu = uu_ref[...]
    row = lax.broadcasted_iota(jnp.int32, (tile, tile), 0)
    col = lax.broadcasted_iota(jnp.int32, (tile, tile), 1)
    below = col < row

    def lanes_of(h):
        lb = h // HEADS_PER_BLOCK
        return slice(lb * LANES, (lb + 1) * LANES)

    qh = []
    for lb in range(lane_blocks):
        qh += _head_queries(-q_ref[0, :, lb * LANES:(lb + 1) * LANES])

    @pl.when(i == 0)
    def _():
        for h in heads:
            a, total = _sb_block(_dot_nt(qh[h], k_ref[0, 0:tile, lanes_of(h)]), None, below, uu)
            acc_ref[h] = _dot(a, v_ref[0, 0:tile, lanes_of(h)])
            car_ref[h] = total

    @pl.when(i > 0)
    def _():
        rows = pl.ds(pl.multiple_of((i - 1) * tile, tile), 2 * tile)
        u = [_dot_nt(qh[h], k_ref[0, rows, lanes_of(h)]) for h in heads]
        s_diag = [_sb_suffix(u[h][:, tile:], below, uu) for h in heads]
        s_prev = [_sb_suffix(u[h][:, :tile], None, uu) for h in heads]
        for h in heads:
            t_diag = _sb_total(s_diag[h])
            a_diag = _sb_weights(u[h][:, tile:], s_diag[h], None, below)
            a_prev = _sb_weights(u[h][:, :tile], s_prev[h], t_diag, None)
            acc_ref[h] = _dot(jnp.concatenate([a_prev, a_diag], axis=1), v_ref[0, rows, lanes_of(h)])
            car_ref[h] = t_diag + _sb_total(s_prev[h])

    def exhausted():
        worst = functools.reduce(jnp.maximum, [car_ref[h] for h in heads])
        return jnp.max(worst) < SB_UNDERFLOW_LOG

    def cond(state):
        j, done = state
        return jnp.logical_and(j >= 0, jnp.logical_not(done))

    def body(state):
        j, _ = state
        rows = pl.ds(pl.multiple_of(j * tile, tile), tile)
        for h in heads:
            a, total = _sb_block(_dot_nt(qh[h], k_ref[0, rows, lanes_of(h)]), car_ref[h], None, uu)
            acc_ref[h] += _dot(a, v_ref[0, rows, lanes_of(h)])
            car_ref[h] += total
        return j - 1, exhausted()

    lax.while_loop(cond, body, (i - 2, exhausted()))
    for lb in range(lane_blocks):
        h0 = lb * HEADS_PER_BLOCK
        o_ref[0, :, lb * LANES:(lb + 1) * LANES] = _merge_heads(acc_ref[h0], acc_ref[h0 + 1]).astype(o_ref.dtype)


def _sb_attention(qkv, uu, tile, lane_blocks):
    b, seq, _ = qkv.shape
    width = lane_blocks * LANES
    blocks = BRANCH_WIDTH // width
    n_heads = lane_blocks * HEADS_PER_BLOCK
    return pl.pallas_call(
        functools.partial(_sb_kernel, tile=tile, lane_blocks=lane_blocks),
        grid=(b, blocks, seq // tile),
        in_specs=[
            pl.BlockSpec((1, tile, width), lambda bi, p, i: (bi, i, p)),
            pl.BlockSpec((1, seq, width), lambda bi, p, i: (bi, 0, blocks + p)),
            pl.BlockSpec((1, seq, width), lambda bi, p, i: (bi, 0, 2 * blocks + p)),
            pl.BlockSpec(uu.shape, lambda bi, p, i: (0, 0)),
        ],
        out_specs=pl.BlockSpec((1, tile, width), lambda bi, p, i: (bi, i, p)),
        out_shape=jax.ShapeDtypeStruct((b, seq, BRANCH_WIDTH), bf16),
        scratch_shapes=[
            pltpu.VMEM((n_heads, tile, LANES), f32),
            pltpu.VMEM((n_heads, tile, LANES), f32),
        ],
        compiler_params=_params("parallel", "parallel", "arbitrary"),
        name="sb_attn",
    )(qkv, qkv, qkv, uu)


def _lane_fold(x, op):
    return functools.reduce(op, [x[:, t * LANES:(t + 1) * LANES] for t in range(x.shape[1] // LANES)])


def _fox_kernel(qa_ref, qb_ref, k_ref, v_ref, fc_ref, oa_ref, ob_ref, qh_ref, s_ref, m_ref, acc_ref,
                *, tile, n_blocks):
    p = pl.program_id(1)
    t = pl.program_id(2)
    n_slots = n_blocks + 1
    row = lax.broadcasted_iota(jnp.int32, (tile, tile), 0)
    col = lax.broadcasted_iota(jnp.int32, (tile, tile), 1)
    causal = col <= row

    for w, q_ref in enumerate((qa_ref, qb_ref)):
        for h, q_head in enumerate(_head_queries(q_ref[0])):
            qh_ref[w, h] = q_head
    m_ref[...] = jnp.full_like(m_ref, NEG_BIG)
    acc_ref[...] = jnp.zeros_like(acc_ref)

    def owner(c):
        if c == 0:
            return 0, t
        if c == n_slots - 1:
            return 1, n_blocks - 1 - t
        is_b = c > t
        return is_b.astype(jnp.int32), jnp.where(is_b, c - t - 1, t - c)

    def scores(c):
        w, kc = owner(c)
        k = k_ref[0, pl.ds(pl.multiple_of(kc * tile, tile), tile), :]
        for h in range(HEADS_PER_BLOCK):
            head = p * HEADS_PER_BLOCK + h
            f_key = fc_ref[0, pl.ds(head * n_blocks + kc, 1), :]
            s = _dot_nt(qh_ref[w, h], k) * LOG2E - f_key * LOG2E
            if c in (0, n_slots - 1):
                s = jnp.where(causal, s, NEG_BIG)
            m_ref[w, h] = jnp.maximum(m_ref[w, h], _lane_fold(s, jnp.maximum))
            s_ref[h, c] = s

    def finish_max(w):
        for h in range(HEADS_PER_BLOCK):
            m_ref[w, h] = jnp.broadcast_to(jnp.max(m_ref[w, h], axis=1, keepdims=True), (tile, LANES))

    for c in range(1, n_slots - 1):
        scores(c)
    scores(0)
    finish_max(0)
    scores(n_slots - 1)
    finish_max(1)

    ones = jnp.ones((tile, LANES), bf16)
    for c in list(range(1, n_slots - 1)) + [0, n_slots - 1]:
        w, kc = owner(c)
        v = jnp.concatenate([v_ref[0, pl.ds(pl.multiple_of(kc * tile, tile), tile), :], ones], axis=1)
        for h in range(HEADS_PER_BLOCK):
            pr = jnp.exp2(s_ref[h, c] - _lane_tile(m_ref[w, h], tile))
            acc_ref[w, h] += _dot(pr.astype(bf16), v)

    for w, o_ref in enumerate((oa_ref, ob_ref)):
        out = [acc_ref[w, h, :, 0:LANES] / acc_ref[w, h, :, LANES:2 * LANES] for h in range(HEADS_PER_BLOCK)]
        o_ref[0] = _merge_heads(out[0], out[1]).astype(o_ref.dtype)


def _fox_attention(qkv, fc, tile):
    b, seq, _ = qkv.shape
    blocks = BRANCH_WIDTH // LANES
    n_blocks = seq // tile
    half = n_blocks // 2
    base = 3 * blocks
    out = jax.ShapeDtypeStruct((b, seq // 2, BRANCH_WIDTH), bf16)
    return pl.pallas_call(
        functools.partial(_fox_kernel, tile=tile, n_blocks=n_blocks),
        grid=(b, blocks, half),
        in_specs=[
            pl.BlockSpec((1, tile, LANES), lambda bi, p, t: (bi, t, base + p)),
            pl.BlockSpec((1, tile, LANES), lambda bi, p, t: (bi, n_blocks - 1 - t, base + p)),
            pl.BlockSpec((1, seq, LANES), lambda bi, p, t: (bi, 0, base + blocks + p)),
            pl.BlockSpec((1, seq, LANES), lambda bi, p, t: (bi, 0, base + 2 * blocks + p)),
            pl.BlockSpec((1, N_HEADS * n_blocks, tile), lambda bi, p, t: (bi, 0, 0)),
        ],
        out_specs=[
            pl.BlockSpec((1, tile, LANES), lambda bi, p, t: (bi, t, p)),
            pl.BlockSpec((1, tile, LANES), lambda bi, p, t: (bi, half - 1 - t, p)),
        ],
        out_shape=[out, out],
        scratch_shapes=[
            pltpu.VMEM((2, HEADS_PER_BLOCK, tile, LANES), bf16),
            pltpu.VMEM((HEADS_PER_BLOCK, n_blocks + 1, tile, tile), f32),
            pltpu.VMEM((2, HEADS_PER_BLOCK, tile, LANES), f32),
            pltpu.VMEM((2, HEADS_PER_BLOCK, tile, 2 * LANES), f32),
        ],
        compiler_params=_params("parallel", "parallel", "arbitrary"),
        name="fox_attn",
    )(qkv, qkv, qkv, qkv, fc)


def _post_kernel(x_ref, sh_ref, sc_ref, g1_ref, asb_ref, afx_lo_ref, afx_hi_ref, wg_ref, bg_ref, wsb_ref,
                 wfx_ref, wo_ref, lg_ref, lb_ref, o_ref, *, alpha, tiles_per_seq):
    x = x_ref[...]
    d = x.shape[1]
    u = (_layer_norm(x) * (1.0 + sc_ref[0]) + sh_ref[0]).astype(bf16)
    y_sb = _dot(asb_ref[...], wsb_ref[...])
    in_first_half = (pl.program_id(0) % tiles_per_seq) < tiles_per_seq // 2
    a_fx = jnp.where(in_first_half, afx_lo_ref[...], afx_hi_ref[...])
    y_fx = _dot(a_fx, wfx_ref[...])
    g_sb = _sigmoid(_dot_nt(u, wg_ref[0:d, :]) + bg_ref[:, 0:d])
    g_fx = _sigmoid(_dot_nt(u, wg_ref[d:2 * d, :]) + bg_ref[:, d:2 * d])
    mixed = (g_sb * y_sb + g_fx * y_fx).astype(bf16)
    mix = _dot(mixed, wo_ref[...])
    r = alpha * x + g1_ref[0] * mix
    o_ref[...] = _layer_norm(r) * lg_ref[...] + lb_ref[...]


def _post(x2, sh, sc, g1, asb, afx_lo, afx_hi, wg, bg, wsb, wfx, wo, lg, lb, seq, alpha):
    n_tok, d = x2.shape
    tile = ROW_TILE
    tiles_per_seq = seq // tile
    half = tiles_per_seq // 2
    bmap = lambda r: (r // tiles_per_seq, 0, 0)
    lo_map = lambda r: ((r // tiles_per_seq) * half + jnp.minimum(r % tiles_per_seq, half - 1), 0)
    hi_map = lambda r: ((r // tiles_per_seq) * half + jnp.maximum(r % tiles_per_seq - half, 0), 0)
    full = lambda a: pl.BlockSpec(a.shape, lambda r: (0,) * a.ndim)
    return pl.pallas_call(
        functools.partial(_post_kernel, alpha=alpha, tiles_per_seq=tiles_per_seq),
        grid=(n_tok // tile,),
        in_specs=[
            pl.BlockSpec((tile, d), lambda r: (r, 0)),
            pl.BlockSpec((1, 1, d), bmap),
            pl.BlockSpec((1, 1, d), bmap),
            pl.BlockSpec((1, 1, d), bmap),
            pl.BlockSpec((tile, BRANCH_WIDTH), lambda r: (r, 0)),
            pl.BlockSpec((tile, BRANCH_WIDTH), lo_map),
            pl.BlockSpec((tile, BRANCH_WIDTH), hi_map),
            full(wg), full(bg), full(wsb), full(wfx), full(wo), full(lg), full(lb),
        ],
        out_specs=pl.BlockSpec((tile, d), lambda r: (r, 0)),
        out_shape=jax.ShapeDtypeStruct((n_tok, d), f32),
        compiler_params=_params("arbitrary"),
        name="post",
    )(x2, sh, sc, g1, asb, afx_lo, afx_hi, wg, bg, wsb, wfx, wo, lg, lb)


def _ffn_kernel(x_ref, sh_ref, sc_ref, g2_ref, wg_ref, wu_ref, wd_ref, lg_ref, lb_ref, o_ref, acc_ref,
                *, alpha):
    x = x_ref[...]
    u = (_layer_norm(x) * (1.0 + sc_ref[0]) + sh_ref[0]).astype(bf16)
    d_ff = wg_ref.shape[1]
    for c0 in range(0, d_ff, FFN_CHUNK):
        gate = _dot(u, wg_ref[:, c0:c0 + FFN_CHUNK].astype(bf16))
        up = _dot(u, wu_ref[:, c0:c0 + FFN_CHUNK].astype(bf16))
        h = (gate * _sigmoid(gate) * up).astype(bf16)
        part = _dot(h, wd_ref[c0:c0 + FFN_CHUNK, :].astype(bf16))
        if c0 == 0:
            acc_ref[...] = part
        else:
            acc_ref[...] += part
    r = alpha * x + g2_ref[0] * acc_ref[...]
    o_ref[...] = _layer_norm(r) * lg_ref[...] + lb_ref[...]


def _ffn(x2, sh, sc, g2, wg, wu, wd, lg, lb, seq, alpha):
    n_tok, d = x2.shape
    tile = ROW_TILE
    tiles_per_seq = seq // tile
    bmap = lambda r: (r // tiles_per_seq, 0, 0)
    resident =lambda a: pl.BlockSpec(a.shape, lambda r: (0,) * a.ndim, pipeline_mode=pl.Buffered(1))
    return pl.pallas_call(
        functools.partial(_ffn_kernel, alpha=alpha),
        grid=(n_tok // tile,),
        in_specs=[
            pl.BlockSpec((tile, d), lambda r: (r, 0)),
            pl.BlockSpec((1, 1, d), bmap),
            pl.BlockSpec((1, 1, d), bmap),
            pl.BlockSpec((1, 1, d), bmap),
            resident(wg), resident(wu), resident(wd), resident(lg), resident(lb),
        ],
        out_specs=pl.BlockSpec((tile, d), lambda r: (r, 0)),
        out_shape=jax.ShapeDtypeStruct((n_tok, d), f32),
        scratch_shapes=[pltpu.VMEM((tile, d), f32)],
        compiler_params=_params("arbitrary"),
        name="ffn",
    )(x2, sh, sc, g2, wg, wu, wd, lg, lb)


def _suffix_sum_matrix(tile):
    r = jnp.arange(tile)[:, None]
    s = jnp.arange(tile)[None, :]
    return (r >= s).astype(bf16)


def _inclusive_upper(chunk):
    r = jnp.arange(chunk)[:, None]
    s = jnp.arange(chunk)[None, :]
    return (r <= s).astype(bf16)


def kernel(x, c, w_ada, b_ada, w_in, b_gate, b_forget, w_sb_out, w_fox_out, w_o, ln1_g, ln1_b,
           w_ffn_gate, w_ffn_up, w_ffn_down, ln2_g, ln2_b):
    batch, seq, d = x.shape
    depth = w_ada.shape[0]
    alpha = (2 * depth) ** 0.25
    n_tok = batch * seq
    qkv_cols = 6 * BRANCH_WIDTH
    off_fgate = qkv_cols
    off_bgate = off_fgate + N_HEADS
    assert w_in.shape[2] == off_bgate + 2 * d
    assert seq % ATTN_TILE == 0 and seq % FOX_TILE == 0 and seq % ROW_TILE == 0

    c_pad = jnp.zeros((8, d), f32).at[:batch].set(c)
    uu = _suffix_sum_matrix(ATTN_TILE)
    tri = _inclusive_upper(256)

    x2 = x.reshape(n_tok, d)
    for l in range(depth):
        ada = _ada(c_pad, w_ada[l], b_ada[l][None, :])[:batch]
        sh1, sc1, g1, sh2, sc2, g2 = [t[:, None, :] for t in jnp.split(ada, 6, axis=-1)]

        w_in_t = jnp.swapaxes(w_in[l], 0, 1)
        bf_pad = jnp.zeros((1, LANES), f32).at[0, :N_HEADS].set(b_forget[l])

        qkv, flog = _inproj(x2, sh1, sc1, w_in_t, seq)
        qkv = qkv.reshape(batch, seq, qkv_cols)
        fc = _fcum(flog.reshape(batch, seq, LANES), bf_pad, tri)
        fc = fc.reshape(batch, N_HEADS * (seq // FOX_TILE), FOX_TILE)

        a_sb = _sb_attention(qkv, uu, ATTN_TILE, SB_LANE_BLOCKS).reshape(n_tok, BRANCH_WIDTH)
        a_fx_lo, a_fx_hi = [a.reshape(n_tok // 2, BRANCH_WIDTH) for a in _fox_attention(qkv, fc, FOX_TILE)]

        x2 = _post(x2, sh1, sc1, g1, a_sb, a_fx_lo, a_fx_hi,
                   w_in_t[off_bgate:].astype(bf16), b_gate[l][None, :],
                   w_sb_out[l].astype(bf16), w_fox_out[l].astype(bf16), w_o[l].astype(bf16),
                   ln1_g[l][None, :], ln1_b[l][None, :], seq, alpha)
        x2 = _ffn(x2, sh2, sc2, g2,
                  w_ffn_gate[l], w_ffn_up[l], w_ffn_down[l],
                  ln2_g[l][None, :], ln2_b[l][None, :], seq, alpha)
    return x2.reshape(batch, seq, d)
```

```python
import functools

import jax
import jax.numpy as jnp
from jax import lax
from jax.experimental import pallas as pl
from jax.experimental.pallas import tpu as pltpu

HEAD_DIM = 64
N_HEADS = 8
BRANCH_WIDTH = N_HEADS * HEAD_DIM
LANES = 128
SUBLANES = 8
HEADS_PER_BLOCK = LANES // HEAD_DIM
LN_EPS = 1e-5
NEG_BIG = -1e30
LOG2E = 1.4426950408889634
VMEM_LIMIT = 56 * 1024 * 1024
SB_UNDERFLOW_LOG = -105.0

ATTN_TILE = 256
SB_LANE_BLOCKS = 4
FOX_TILE = 512
ROW_TILE = 512
FFN_CHUNK = 256
ADA_COLS = 1024
FCUM_CHUNK = 256

f32 = jnp.float32
bf16 = jnp.bfloat16


def _dot(a, b):
    return jnp.dot(a, b, preferred_element_type=f32)


def _dot_nt(a, b):
    return lax.dot_general(a, b, (((1,), (1,)), ((), ())), preferred_element_type=f32)


def _layer_norm(x):
    mu = jnp.mean(x, axis=-1, keepdims=True)
    xc = x - mu
    var = jnp.mean(xc * xc, axis=-1, keepdims=True)
    return xc * lax.rsqrt(var + LN_EPS)


def _sigmoid(x):
    return 1.0 / (1.0 + jnp.exp(-x))


def _params(*sem):
    return pltpu.CompilerParams(dimension_semantics=sem, vmem_limit_bytes=VMEM_LIMIT)


def _ada_kernel(c_ref, w_ref, b_ref, o_ref):
    c = c_ref[...]
    c_act = (c * _sigmoid(c)).astype(bf16)
    o_ref[...] = _dot(c_act, w_ref[...].astype(bf16)) + b_ref[...]


def _ada(c_pad, w, b):
    rows, d = c_pad.shape
    n = w.shape[1]
    return pl.pallas_call(
        _ada_kernel,
        grid=(n // ADA_COLS,),
        in_specs=[
            pl.BlockSpec((rows, d), lambda j: (0, 0)),
            pl.BlockSpec((d, ADA_COLS), lambda j: (0, j)),
            pl.BlockSpec((1, ADA_COLS), lambda j: (0, j)),
        ],
        out_specs=pl.BlockSpec((rows, ADA_COLS), lambda j: (0, j)),
        out_shape=jax.ShapeDtypeStruct((rows, n), f32),
        compiler_params=_params("arbitrary"),
        name="ada",
    )(c_pad, w, b)


def _inproj_kernel(x_ref, sh_ref, sc_ref, wqkv_ref, wf_ref, qkv_ref, fl_ref):
    u = (_layer_norm(x_ref[...]) * (1.0 + sc_ref[0]) + sh_ref[0]).astype(bf16)
    for g in range(wqkv_ref.shape[0] // BRANCH_WIDTH):
        cols = slice(g * BRANCH_WIDTH, (g + 1) * BRANCH_WIDTH)
        y = _dot_nt(u, wqkv_ref[cols, :].astype(bf16))
        if g % 3 == 0:
            y = y * HEAD_DIM ** -0.5
        qkv_ref[:, cols] = y.astype(bf16)
    fl_ref[...] = _dot_nt(u, wf_ref[...].astype(bf16))


def _inproj(x2, sh, sc, w_in_t, seq):
    n_tok, d = x2.shape
    qkv_cols = 6 * BRANCH_WIDTH
    tiles_per_seq = seq // ROW_TILE
    bmap = lambda r: (r // tiles_per_seq, 0, 0)
    return pl.pallas_call(
        _inproj_kernel,
        grid=(n_tok // ROW_TILE,),
        in_specs=[
            pl.BlockSpec((ROW_TILE, d), lambda r: (r, 0)),
            pl.BlockSpec((1, 1, d), bmap),
            pl.BlockSpec((1, 1, d), bmap),
            pl.BlockSpec((qkv_cols, d), lambda r: (0, 0), pipeline_mode=pl.Buffered(1)),
            pl.BlockSpec((LANES, d), lambda r: (qkv_cols // LANES, 0), pipeline_mode=pl.Buffered(1)),
        ],
        out_specs=[
            pl.BlockSpec((ROW_TILE, qkv_cols), lambda r: (r, 0)),
            pl.BlockSpec((ROW_TILE, LANES), lambda r: (r, 0)),
        ],
        out_shape=[
            jax.ShapeDtypeStruct((n_tok, qkv_cols), bf16),
            jax.ShapeDtypeStruct((n_tok, LANES), f32),
        ],
        compiler_params=_params("arbitrary"),
        name="inproj",
    )(x2, sh, sc, w_in_t, w_in_t)


def _split3(x):
    hi = x.astype(bf16)
    r1 = x - hi.astype(f32)
    mid = r1.astype(bf16)
    lo = (r1 - mid.astype(f32)).astype(bf16)
    return hi, mid, lo


def _fcum_kernel(fl_ref, bf_ref, tri_ref, o_ref, *, chunk):
    seq = fl_ref.shape[1]
    tri = tri_ref[...]
    carry = jnp.zeros((N_HEADS, 1), f32)
    for c0 in range(0, seq, chunk):
        logit = fl_ref[0, c0:c0 + chunk, :] + bf_ref[...]
        ls = jnp.minimum(logit, 0.0) - jnp.log1p(jnp.exp(-jnp.abs(logit)))
        ls_t = ls.T[0:N_HEADS, :]
        hi, mid, lo = _split3(ls_t)
        cs = _dot(hi, tri) + _dot(mid, tri) + _dot(lo, tri) + carry
        o_ref[0, :, c0:c0 + chunk] = cs
        carry = cs[:, chunk - 1:chunk]


def _fcum(fl3, bf_pad, tri):
    b, seq, _ = fl3.shape
    chunk = tri.shape[0]
    return pl.pallas_call(
        functools.partial(_fcum_kernel, chunk=chunk),
        grid=(b,),
        in_specs=[
            pl.BlockSpec((1, seq, LANES), lambda i: (i, 0, 0)),
            pl.BlockSpec((1, LANES), lambda i: (0, 0)),
            pl.BlockSpec(tri.shape, lambda i: (0, 0)),
        ],
        out_specs=pl.BlockSpec((1, N_HEADS, seq), lambda i: (i, 0, 0)),
        out_shape=jax.ShapeDtypeStruct((b, N_HEADS, seq), f32),
        compiler_params=_params("arbitrary"),
        name="fcum",
    )(fl3, bf_pad, tri)


def _head_queries(q):
    lane = lax.broadcasted_iota(jnp.int32, q.shape, 1)
    zero = jnp.zeros_like(q)
    return [jnp.where(lane < HEAD_DIM, q, zero), jnp.where(lane >= HEAD_DIM, q, zero)]


def _lane_tile(x, width):
    reps = width // LANES
    return x if reps == 1 else jnp.concatenate([x] * reps, axis=1)


def _merge_heads(o0, o1):
    lane = lax.broadcasted_iota(jnp.int32, o0.shape, 1)
    return jnp.where(lane < HEAD_DIM, o0, o1)


def _sb_suffix(u, mask, uu):
    log_1m = jnp.minimum(u, 0.0) - jnp.log(1.0 + jnp.exp(-jnp.abs(u)))
    if mask is not None:
        log_1m = jnp.where(mask, log_1m, 0.0)
    return _dot(log_1m.astype(bf16), uu)


def _sb_total(incl):
    return jnp.broadcast_to(incl[:, 0:1], (incl.shape[0], LANES))


def _sb_weights(u, incl, carry, mask):
    g = incl if carry is None else incl + _lane_tile(carry, u.shape[1])
    a = jnp.exp(g - u)
    if mask is not None:
        a = jnp.where(mask, a, 0.0)
    return a.astype(bf16)


def _sb_block(u, carry, mask, uu):
    incl = _sb_suffix(u, mask, uu)
    return _sb_weights(u, incl, carry, mask), _sb_total(incl)


def _sb_kernel(q_ref, k_ref, v_ref, uu_ref, o_ref, acc_ref, car_ref, *, tile, lane_blocks):
    i = pl.program_id(2)
    n_heads = lane_blocks * HEADS_PER_BLOCK
    heads = range(n_heads)
    uu = uu_ref[...]
    row = lax.broadcasted_iota(jnp.int32, (tile, tile), 0)
    col = lax.broadcasted_iota(jnp.int32, (tile, tile), 1)
    below = col < row

    def lanes_of(h):
        lb = h // HEADS_PER_BLOCK
        return slice(lb * LANES, (lb + 1) * LANES)

    qh = []
    for lb in range(lane_blocks):
        qh += _head_queries(-q_ref[0, :, lb * LANES:(lb + 1) * LANES])

    @pl.when(i == 0)
    def _():
        for h in heads:
            a, total = _sb_block(_dot_nt(qh[h], k_ref[0, 0:tile, lanes_of(h)]), None, below, uu)
            acc_ref[h] = _dot(a, v_ref[0, 0:tile, lanes_of(h)])
            car_ref[h] = total

    @pl.when(i > 0)
    def _():
        rows = pl.ds(pl.multiple_of((i - 1) * tile, tile), 2 * tile)
        u = [_dot_nt(qh[h], k_ref[0, rows, lanes_of(h)]) for h in heads]
        s_diag = [_sb_suffix(u[h][:, tile:], below, uu) for h in heads]
        s_prev = [_sb_suffix(u[h][:, :tile], None, uu) for h in heads]
        for h in heads:
            t_diag = _sb_total(s_diag[h])
            a_diag = _sb_weights(u[h][:, tile:], s_diag[h], None, below)
            a_prev = _sb_weights(u[h][:, :tile], s_prev[h], t_diag, None)
            acc_ref[h] = _dot(jnp.concatenate([a_prev, a_diag], axis=1), v_ref[0, rows, lanes_of(h)])
            car_ref[h] = t_diag + _sb_total(s_prev[h])

    def exhausted():
        worst = functools.reduce(jnp.maximum, [car_ref[h] for h in heads])
        return jnp.max(worst) < SB_UNDERFLOW_LOG

    def cond(state):
        j, done = state
        return jnp.logical_and(j >= 0, jnp.logical_not(done))

    def body(state):
        j, _ = state
        rows = pl.ds(pl.multiple_of(j * tile, tile), tile)
        for h in heads:
            a, total = _sb_block(_dot_nt(qh[h], k_ref[0, rows, lanes_of(h)]), car_ref[h], None, uu)
            acc_ref[h] += _dot(a, v_ref[0, rows, lanes_of(h)])
            car_ref[h] += total
        return j - 1, exhausted()

    lax.while_loop(cond, body, (i - 2, exhausted()))
    for lb in range(lane_blocks):
        h0 = lb * HEADS_PER_BLOCK
        o_ref[0, :, lb * LANES:(lb + 1) * LANES] = _merge_heads(acc_ref[h0], acc_ref[h0 + 1]).astype(o_ref.dtype)


def _sb_attention(qkv, uu, tile, lane_blocks):
    b, seq, _ = qkv.shape
    width = lane_blocks * LANES
    blocks = BRANCH_WIDTH // width
    n_heads = lane_blocks * HEADS_PER_BLOCK
    return pl.pallas_call(
        functools.partial(_sb_kernel, tile=tile, lane_blocks=lane_blocks),
        grid=(b, blocks, seq // tile),
        in_specs=[
            pl.BlockSpec((1, tile, width), lambda bi, p, i: (bi, i, p)),
            pl.BlockSpec((1, seq, width), lambda bi, p, i: (bi, 0, blocks + p)),
            pl.BlockSpec((1, seq, width), lambda bi, p, i: (bi, 0, 2 * blocks + p)),
            pl.BlockSpec(uu.shape, lambda bi, p, i: (0, 0)),
        ],
        out_specs=pl.BlockSpec((1, tile, width), lambda bi, p, i: (bi, i, p)),
        out_shape=jax.ShapeDtypeStruct((b, seq, BRANCH_WIDTH), bf16),
        scratch_shapes=[
            pltpu.VMEM((n_heads, tile, LANES), f32),
            pltpu.VMEM((n_heads, tile, LANES), f32),
        ],
        compiler_params=_params("parallel", "parallel", "arbitrary"),
        name="sb_attn",
    )(qkv, qkv, qkv, uu)


def _lane_fold(x, op):
    return functools.reduce(op, [x[:, t * LANES:(t + 1) * LANES] for t in range(x.shape[1] // LANES)])


def _fox_kernel(qa_ref, qb_ref, k_ref, v_ref, fc_ref, oa_ref, ob_ref, qh_ref, s_ref, m_ref, acc_ref,
                *, tile, n_blocks):
    p = pl.program_id(1)
    t = pl.program_id(2)
    n_slots = n_blocks + 1
    row = lax.broadcasted_iota(jnp.int32, (tile, tile), 0)
    col = lax.broadcasted_iota(jnp.int32, (tile, tile), 1)
    causal = col <= row

    for w, q_ref in enumerate((qa_ref, qb_ref)):
        for h, q_head in enumerate(_head_queries(q_ref[0])):
            qh_ref[w, h] = q_head
    m_ref[...] = jnp.full_like(m_ref, NEG_BIG)
    acc_ref[...] = jnp.zeros_like(acc_ref)

    def owner(c):
        if c == 0:
            return 0, t
        if c == n_slots - 1:
            return 1, n_blocks - 1 - t
        is_b = c > t
        return is_b.astype(jnp.int32), jnp.where(is_b, c - t - 1, t - c)

    def scores(c):
        w, kc = owner(c)
        k = k_ref[0, pl.ds(pl.multiple_of(kc * tile, tile), tile), :]
        for h in range(HEADS_PER_BLOCK):
            head = p * HEADS_PER_BLOCK + h
            f_key = fc_ref[0, pl.ds(head * n_blocks + kc, 1), :]
            s = _dot_nt(qh_ref[w, h], k) * LOG2E - f_key * LOG2E
            if c in (0, n_slots - 1):
                s = jnp.where(causal, s, NEG_BIG)
            m_ref[w, h] = jnp.maximum(m_ref[w, h], _lane_fold(s, jnp.maximum))
            s_ref[h, c] = s

    def finish_max(w):
        for h in range(HEADS_PER_BLOCK):
            m_ref[w, h] = jnp.broadcast_to(jnp.max(m_ref[w, h], axis=1, keepdims=True), (tile, LANES))

    for c in range(1, n_slots - 1):
        scores(c)
    scores(0)
    finish_max(0)
    scores(n_slots - 1)
    finish_max(1)

    ones = jnp.ones((tile, LANES), bf16)
    for c in list(range(1, n_slots - 1)) + [0, n_slots - 1]:
        w, kc = owner(c)
        v = jnp.concatenate([v_ref[0, pl.ds(pl.multiple_of(kc * tile, tile), tile), :], ones], axis=1)
        for h in range(HEADS_PER_BLOCK):
            pr = jnp.exp2(s_ref[h, c] - _lane_tile(m_ref[w, h], tile))
            acc_ref[w, h] += _dot(pr.astype(bf16), v)

    for w, o_ref in enumerate((oa_ref, ob_ref)):
        out = [acc_ref[w, h, :, 0:LANES] / acc_ref[w, h, :, LANES:2 * LANES] for h in range(HEADS_PER_BLOCK)]
        o_ref[0] = _merge_heads(out[0], out[1]).astype(o_ref.dtype)


def _fox_attention(qkv, fc, tile):
    b, seq, _ = qkv.shape
    blocks = BRANCH_WIDTH // LANES
    n_blocks = seq // tile
    half = n_blocks // 2
    base = 3 * blocks
    out = jax.ShapeDtypeStruct((b, seq // 2, BRANCH_WIDTH), bf16)
    return pl.pallas_call(
        functools.partial(_fox_kernel, tile=tile, n_blocks=n_blocks),
        grid=(b, blocks, half),
        in_specs=[
            pl.BlockSpec((1, tile, LANES), lambda bi, p, t: (bi, t, base + p)),
            pl.BlockSpec((1, tile, LANES), lambda bi, p, t: (bi, n_blocks - 1 - t, base + p)),
            pl.BlockSpec((1, seq, LANES), lambda bi, p, t: (bi, 0, base + blocks + p)),
            pl.BlockSpec((1, seq, LANES), lambda bi, p, t: (bi, 0, base + 2 * blocks + p)),
            pl.BlockSpec((1, N_HEADS * n_blocks, tile), lambda bi, p, t: (bi, 0, 0)),
        ],
        out_specs=[
            pl.BlockSpec((1, tile, LANES), lambda bi, p, t: (bi, t, p)),
            pl.BlockSpec((1, tile, LANES), lambda bi, p, t: (bi, half - 1 - t, p)),
        ],
        out_shape=[out, out],
        scratch_shapes=[
            pltpu.VMEM((2, HEADS_PER_BLOCK, tile, LANES), bf16),
            pltpu.VMEM((HEADS_PER_BLOCK, n_blocks + 1, tile, tile), f32),
            pltpu.VMEM((2, HEADS_PER_BLOCK, tile, LANES), f32),
            pltpu.VMEM((2, HEADS_PER_BLOCK, tile, 2 * LANES), f32),
        ],
        compiler_params=_params("parallel", "parallel", "arbitrary"),
        name="fox_attn",
    )(qkv, qkv, qkv, qkv, fc)


def _post_kernel(x_ref, sh_ref, sc_ref, g1_ref, asb_ref, afx_lo_ref, afx_hi_ref, wg_ref, bg_ref, wsb_ref,
                 wfx_ref, wo_ref, lg_ref, lb_ref, o_ref, *, alpha, tiles_per_seq):
    x = x_ref[...]
    d = x.shape[1]
    u = (_layer_norm(x) * (1.0 + sc_ref[0]) + sh_ref[0]).astype(bf16)
    y_sb = _dot(asb_ref[...], wsb_ref[...])
    in_first_half = (pl.program_id(0) % tiles_per_seq) < tiles_per_seq // 2
    a_fx = jnp.where(in_first_half, afx_lo_ref[...], afx_hi_ref[...])
    y_fx = _dot(a_fx, wfx_ref[...])
    g_sb = _sigmoid(_dot_nt(u, wg_ref[0:d, :]) + bg_ref[:, 0:d])
    g_fx = _sigmoid(_dot_nt(u, wg_ref[d:2 * d, :]) + bg_ref[:, d:2 * d])
    mixed = (g_sb * y_sb + g_fx * y_fx).astype(bf16)
    mix = _dot(mixed, wo_ref[...])
    r = alpha * x + g1_ref[0] * mix
    o_ref[...] = _layer_norm(r) * lg_ref[...] + lb_ref[...]


def _post(x2, sh, sc, g1, asb, afx_lo, afx_hi, wg, bg, wsb, wfx, wo, lg, lb, seq, alpha):
    n_tok, d = x2.shape
    tile = ROW_TILE
    tiles_per_seq = seq // tile
    half = tiles_per_seq // 2
    bmap = lambda r: (r // tiles_per_seq, 0, 0)
    lo_map = lambda r: ((r // tiles_per_seq) * half + jnp.minimum(r % tiles_per_seq, half - 1), 0)
    hi_map = lambda r: ((r // tiles_per_seq) * half + jnp.maximum(r % tiles_per_seq - half, 0), 0)
    full = lambda a: pl.BlockSpec(a.shape, lambda r: (0,) * a.ndim)
    return pl.pallas_call(
        functools.partial(_post_kernel, alpha=alpha, tiles_per_seq=tiles_per_seq),
        grid=(n_tok // tile,),
        in_specs=[
            pl.BlockSpec((tile, d), lambda r: (r, 0)),
            pl.BlockSpec((1, 1, d), bmap),
            pl.BlockSpec((1, 1, d), bmap),
            pl.BlockSpec((1, 1, d), bmap),
            pl.BlockSpec((tile, BRANCH_WIDTH), lambda r: (r, 0)),
            pl.BlockSpec((tile, BRANCH_WIDTH), lo_map),
            pl.BlockSpec((tile, BRANCH_WIDTH), hi_map),
            full(wg), full(bg), full(wsb), full(wfx), full(wo), full(lg), full(lb),
        ],
        out_specs=pl.BlockSpec((tile, d), lambda r: (r, 0)),
        out_shape=jax.ShapeDtypeStruct((n_tok, d), f32),
        compiler_params=_params("arbitrary"),
        name="post",
    )(x2, sh, sc, g1, asb, afx_lo, afx_hi, wg, bg, wsb, wfx, wo, lg, lb)


def _ffn_kernel(x_ref, sh_ref, sc_ref, g2_ref, wg_ref, wu_ref, wd_ref, lg_ref, lb_ref, o_ref, acc_ref,
                *, alpha):
    x = x_ref[...]
    u = (_layer_norm(x) * (1.0 + sc_ref[0]) + sh_ref[0]).astype(bf16)
    d_ff = wg_ref.shape[1]
    for c0 in range(0, d_ff, FFN_CHUNK):
        gate = _dot(u, wg_ref[:, c0:c0 + FFN_CHUNK].astype(bf16))
        up = _dot(u, wu_ref[:, c0:c0 + FFN_CHUNK].astype(bf16))
        h = (gate * _sigmoid(gate) * up).astype(bf16)
        part = _dot(h, wd_ref[c0:c0 + FFN_CHUNK, :].astype(bf16))
        if c0 == 0:
            acc_ref[...] = part
        else:
            acc_ref[...] += part
    r = alpha * x + g2_ref[0] * acc_ref[...]
    o_ref[...] = _layer_norm(r) * lg_ref[...] + lb_ref[...]


def _ffn(x2, sh, sc, g2, wg, wu, wd, lg, lb, seq, alpha):
    n_tok, d = x2.shape
    tile = ROW_TILE
    tiles_per_seq = seq // tile
    bmap = lambda r: (r // tiles_per_seq, 0, 0)
    resident = lambda a: pl.BlockSpec(a.shape, lambda r: (0,) * a.ndim, pipeline_mode=pl.Buffered(1))
    return pl.pallas_call(
        functools.partial(_ffn_kernel, alpha=alpha),
        grid=(n_tok // tile,),
        in_specs=[
            pl.BlockSpec((tile, d), lambda r: (r, 0)),
            pl.BlockSpec((1, 1, d), bmap),
            pl.BlockSpec((1, 1, d), bmap),
            pl.BlockSpec((1, 1, d), bmap),
            resident(wg), resident(wu), resident(wd), resident(lg), resident(lb),
        ],
        out_specs=pl.BlockSpec((tile, d), lambda r: (r, 0)),
        out_shape=jax.ShapeDtypeStruct((n_tok, d), f32),
        scratch_shapes=[pltpu.VMEM((tile, d), f32)],
        compiler_params=_params("arbitrary"),
        name="ffn",
    )(x2, sh, sc, g2, wg, wu, wd, lg, lb)


def _suffix_sum_matrix(tile):
    r = jnp.arange(tile)[:, None]
    s = jnp.arange(tile)[None, :]
    return (r >= s).astype(bf16)


def _inclusive_upper(chunk):
    r = jnp.arange(chunk)[:, None]
    s = jnp.arange(chunk)[None, :]
    return (r <= s).astype(bf16)


def kernel(x, c, w_ada, b_ada, w_in, b_gate, b_forget, w_sb_out, w_fox_out, w_o, ln1_g, ln1_b,
           w_ffn_gate, w_ffn_up, w_ffn_down, ln2_g, ln2_b):
    batch, seq, d = x.shape
    depth = w_ada.shape[0]
    alpha = (2 * depth) ** 0.25
    n_tok = batch * seq
    qkv_cols = 6 * BRANCH_WIDTH
    off_fgate = qkv_cols
    off_bgate = off_fgate + N_HEADS
    assert w_in.shape[2] == off_bgate + 2 * d
    assert seq % ATTN_TILE == 0 and seq % (2 * FOX_TILE) == 0 and seq % ROW_TILE == 0
    assert seq % FCUM_CHUNK == 0 and batch <= SUBLANES

    c_pad = jnp.zeros((SUBLANES, d), f32).at[:batch].set(c)
    uu = _suffix_sum_matrix(ATTN_TILE)
    tri = _inclusive_upper(FCUM_CHUNK)

    x2 = x.reshape(n_tok, d)
    for l in range(depth):
        ada = _ada(c_pad, w_ada[l], b_ada[l][None, :])[:batch]
        sh1, sc1, g1, sh2, sc2, g2 = [t[:, None, :] for t in jnp.split(ada, 6, axis=-1)]

        w_in_t = jnp.swapaxes(w_in[l], 0, 1)
        bf_pad = jnp.zeros((1, LANES), f32).at[0, :N_HEADS].set(b_forget[l])

        qkv, flog = _inproj(x2, sh1, sc1, w_in_t, seq)
        qkv = qkv.reshape(batch, seq, qkv_cols)
        fc = _fcum(flog.reshape(batch, seq, LANES), bf_pad, tri)
        fc = fc.reshape(batch, N_HEADS * (seq // FOX_TILE), FOX_TILE)

        a_sb = _sb_attention(qkv, uu, ATTN_TILE, SB_LANE_BLOCKS).reshape(n_tok, BRANCH_WIDTH)
        a_fx_lo, a_fx_hi = [a.reshape(n_tok // 2, BRANCH_WIDTH) for a in _fox_attention(qkv, fc, FOX_TILE)]

        x2 = _post(x2, sh1, sc1, g1, a_sb, a_fx_lo, a_fx_hi,
                   w_in_t[off_bgate:].astype(bf16), b_gate[l][None, :],
                   w_sb_out[l].astype(bf16), w_fox_out[l].astype(bf16), w_o[l].astype(bf16),
                   ln1_g[l][None, :], ln1_b[l][None, :], seq, alpha)
        x2 = _ffn(x2, sh2, sc2, g2,
                  w_ffn_gate[l], w_ffn_up[l], w_ffn_down[l],
                  ln2_g[l][None, :], ln2_b[l][None, :], seq, alpha)
    return x2.reshape(batch, seq, d)
```

```python
import functools

import jax
import jax.numpy as jnp
from jax import lax
from jax.experimental import pallas as pl
from jax.experimental.pallas import tpu as pltpu

HEAD_DIM = 64
N_HEADS = 8
BRANCH_WIDTH = N_HEADS * HEAD_DIM
LANES = 128
SUBLANES = 8
HEADS_PER_BLOCK = LANES // HEAD_DIM
LN_EPS = 1e-5
NEG_BIG = -1e30
LOG2E = 1.4426950408889634
VMEM_LIMIT = 56 * 1024 * 1024
SB_UNDERFLOW_LOG = -105.0

ATTN_TILE = 256
SB_LANE_BLOCKS = 4
FOX_TILE = 512
ROW_TILE = 512
FFN_CHUNK = 256
ADA_COLS = 1024
FCUM_CHUNK = 256

f32 = jnp.float32
bf16 = jnp.bfloat16


def _dot(a, b):
    return jnp.dot(a, b, preferred_element_type=f32)


def _dot_nt(a, b):
    return lax.dot_general(a, b, (((1,), (1,)), ((), ())), preferred_element_type=f32)


def _layer_norm(x):
    mu = jnp.mean(x, axis=-1, keepdims=True)
    xc = x - mu
    var = jnp.mean(xc * xc, axis=-1, keepdims=True)
    return xc * lax.rsqrt(var + LN_EPS)


def _sigmoid(x):
    return 1.0 / (1.0 + jnp.exp(-x))


def _params(*sem):
    return pltpu.CompilerParams(dimension_semantics=sem, vmem_limit_bytes=VMEM_LIMIT)


def _ada_kernel(c_ref, w_ref, b_ref, o_ref):
    c = c_ref[...]
    c_act = (c * _sigmoid(c)).astype(bf16)
    o_ref[...] = _dot(c_act, w_ref[...].astype(bf16)) + b_ref[...]


def _ada(c_pad, w, b):
    rows, d = c_pad.shape
    n = w.shape[1]
    return pl.pallas_call(
        _ada_kernel,
        grid=(n // ADA_COLS,),
        in_specs=[
            pl.BlockSpec((rows, d), lambda j: (0, 0)),
            pl.BlockSpec((d, ADA_COLS), lambda j: (0, j)),
            pl.BlockSpec((1, ADA_COLS), lambda j: (0, j)),
        ],
        out_specs=pl.BlockSpec((rows, ADA_COLS), lambda j: (0, j)),
        out_shape=jax.ShapeDtypeStruct((rows, n), f32),
        compiler_params=_params("arbitrary"),
        name="ada",
    )(c_pad, w, b)


def _inproj_kernel(x_ref, sh_ref, sc_ref, wqkv_ref, wf_ref, qkv_ref, fl_ref):
    u = (_layer_norm(x_ref[...]) * (1.0 + sc_ref[0]) + sh_ref[0]).astype(bf16)
    for g in range(wqkv_ref.shape[0] // BRANCH_WIDTH):
        cols = slice(g * BRANCH_WIDTH, (g + 1) * BRANCH_WIDTH)
        y = _dot_nt(u, wqkv_ref[cols, :].astype(bf16))
        if g % 3 == 0:
            y = y * HEAD_DIM ** -0.5
        qkv_ref[:, cols] = y.astype(bf16)
    fl_ref[...] = _dot_nt(u, wf_ref[...].astype(bf16))


def _inproj(x2, sh, sc, w_in_t, seq):
    n_tok, d = x2.shape
    qkv_cols = 6 * BRANCH_WIDTH
    tiles_per_seq = seq // ROW_TILE
    bmap = lambda r: (r // tiles_per_seq, 0, 0)
    return pl.pallas_call(
        _inproj_kernel,
        grid=(n_tok // ROW_TILE,),
        in_specs=[
            pl.BlockSpec((ROW_TILE, d), lambda r: (r, 0)),
            pl.BlockSpec((1, 1, d), bmap),
            pl.BlockSpec((1, 1, d), bmap),
            pl.BlockSpec((qkv_cols, d), lambda r: (0, 0), pipeline_mode=pl.Buffered(1)),
            pl.BlockSpec((LANES, d), lambda r: (qkv_cols // LANES, 0), pipeline_mode=pl.Buffered(1)),
        ],
        out_specs=[
            pl.BlockSpec((ROW_TILE, qkv_cols), lambda r: (r, 0)),
            pl.BlockSpec((ROW_TILE, LANES), lambda r: (r, 0)),
        ],
        out_shape=[
            jax.ShapeDtypeStruct((n_tok, qkv_cols), bf16),
            jax.ShapeDtypeStruct((n_tok, LANES), f32),
        ],
        compiler_params=_params("arbitrary"),
        name="inproj",
    )(x2, sh, sc, w_in_t, w_in_t)


def _split3(x):
    hi = x.astype(bf16)
    r1 = x - hi.astype(f32)
    mid = r1.astype(bf16)
    lo = (r1 - mid.astype(f32)).astype(bf16)
    return hi, mid, lo


def _fcum_kernel(fl_ref, bf_ref, tri_ref, o_ref, *, chunk):
    seq = fl_ref.shape[1]
    tri = tri_ref[...]
    carry = jnp.zeros((N_HEADS, 1), f32)
    for c0 in range(0, seq, chunk):
        logit = fl_ref[0, c0:c0 + chunk, :] + bf_ref[...]
        ls = jnp.minimum(logit, 0.0) - jnp.log1p(jnp.exp(-jnp.abs(logit)))
        ls_t = ls.T[0:N_HEADS, :]
        hi, mid, lo = _split3(ls_t)
        cs = _dot(hi, tri) + _dot(mid, tri) + _dot(lo, tri) + carry
        o_ref[0, :, c0:c0 + chunk] = cs
        carry = cs[:, chunk - 1:chunk]


def _fcum(fl3, bf_pad, tri):
    b, seq, _ = fl3.shape
    chunk = tri.shape[0]
    return pl.pallas_call(
        functools.partial(_fcum_kernel, chunk=chunk),
        grid=(b,),
        in_specs=[
            pl.BlockSpec((1, seq, LANES), lambda i: (i, 0, 0)),
            pl.BlockSpec((1, LANES), lambda i: (0, 0)),
            pl.BlockSpec(tri.shape, lambda i: (0, 0)),
        ],
        out_specs=pl.BlockSpec((1, N_HEADS, seq), lambda i: (i, 0, 0)),
        out_shape=jax.ShapeDtypeStruct((b, N_HEADS, seq), f32),
        compiler_params=_params("arbitrary"),
        name="fcum",
    )(fl3, bf_pad, tri)


def _head_queries(q):
    lane = lax.broadcasted_iota(jnp.int32, q.shape, 1)
    zero = jnp.zeros_like(q)
    return [jnp.where(lane < HEAD_DIM, q, zero), jnp.where(lane >= HEAD_DIM, q, zero)]


def _lane_tile(x, width):
    reps = width // LANES
    return x if reps == 1 else jnp.concatenate([x] * reps, axis=1)


def _merge_heads(o0, o1):
    lane = lax.broadcasted_iota(jnp.int32, o0.shape, 1)
    return jnp.where(lane < HEAD_DIM, o0, o1)


def _sb_suffix(u, mask, uu):
    log_1m = jnp.minimum(u, 0.0) - jnp.log(1.0 + jnp.exp(-jnp.abs(u)))
    if mask is not None:
        log_1m = jnp.where(mask, log_1m, 0.0)
    hi = log_1m.astype(bf16)
    lo = (log_1m - hi.astype(f32)).astype(bf16)
    return _dot(jnp.concatenate([hi, lo], axis=1), uu)


def _sb_total(incl):
    return jnp.broadcast_to(incl[:, 0:1], (incl.shape[0], LANES))


def _sb_weights(u, incl, carry, mask):
    g = incl if carry is None else incl + _lane_tile(carry, u.shape[1])
    a = jnp.exp(g - u)
    if mask is not None:
        a = jnp.where(mask, a, 0.0)
    return a.astype(bf16)


def _sb_block(u, carry, mask, uu):
    incl = _sb_suffix(u, mask, uu)
    return _sb_weights(u, incl, carry, mask), _sb_total(incl)


def _sb_kernel(q_ref, k_ref, v_ref, uu_ref, o_ref, acc_ref, car_ref, *, tile, lane_blocks):
    i = pl.program_id(2)
    n_heads = lane_blocks * HEADS_PER_BLOCK
    heads = range(n_heads)
    uu = uu_ref[...]
    row = lax.broadcasted_iota(jnp.int32, (tile, tile), 0)
    col = lax.broadcasted_iota(jnp.int32, (tile, tile), 1)
    below = col < row

    def lanes_of(h):
        lb = h // HEADS_PER_BLOCK
        return slice(lb * LANES, (lb + 1) * LANES)

    qh = []
    for lb in range(lane_blocks):
        qh += _head_queries(-q_ref[0, :, lb * LANES:(lb + 1) * LANES])

    @pl.when(i == 0)
    def _():
        for h in heads:
            a, total = _sb_block(_dot_nt(qh[h], k_ref[0, 0:tile, lanes_of(h)]), None, below, uu)
            acc_ref[h] = _dot(a, v_ref[0, 0:tile, lanes_of(h)])
            car_ref[h] = total

    @pl.when(i > 0)
    def _():
        rows = pl.ds(pl.multiple_of((i - 1) * tile, tile), 2 * tile)
        u = [_dot_nt(qh[h], k_ref[0, rows, lanes_of(h)]) for h in heads]
        s_diag = [_sb_suffix(u[h][:, tile:], below, uu) for h in heads]
        s_prev = [_sb_suffix(u[h][:, :tile], None, uu) for h in heads]
        for h in heads:
            t_diag = _sb_total(s_diag[h])
            a_diag = _sb_weights(u[h][:, tile:], s_diag[h], None, below)
            a_prev = _sb_weights(u[h][:, :tile], s_prev[h], t_diag, None)
            acc_ref[h] = _dot(jnp.concatenate([a_prev, a_diag], axis=1), v_ref[0, rows, lanes_of(h)])
            car_ref[h] = t_diag + _sb_total(s_prev[h])

    def exhausted():
        worst = functools.reduce(jnp.maximum, [car_ref[h] for h in heads])
        return jnp.max(worst) < SB_UNDERFLOW_LOG

    def cond(state):
        j, done = state
        return jnp.logical_and(j >= 0, jnp.logical_not(done))

    def body(state):
        j, _ = state
        rows = pl.ds(pl.multiple_of(j * tile, tile), tile)
        for h in heads:
            a, total = _sb_block(_dot_nt(qh[h], k_ref[0, rows, lanes_of(h)]), car_ref[h], None, uu)
            acc_ref[h] += _dot(a, v_ref[0, rows, lanes_of(h)])
            car_ref[h] += total
        return j - 1, exhausted()

    lax.while_loop(cond, body, (i - 2, exhausted()))
    for lb in range(lane_blocks):
        h0 = lb * HEADS_PER_BLOCK
        o_ref[0, :, lb * LANES:(lb + 1) * LANES] = _merge_heads(acc_ref[h0], acc_ref[h0 + 1]).astype(o_ref.dtype)


def _sb_attention(qkv, uu, tile, lane_blocks):
    b, seq, _ = qkv.shape
    width = lane_blocks * LANES
    blocks = BRANCH_WIDTH // width
    n_heads = lane_blocks * HEADS_PER_BLOCK
    return pl.pallas_call(
        functools.partial(_sb_kernel, tile=tile, lane_blocks=lane_blocks),
        grid=(b, blocks, seq // tile),
        in_specs=[
            pl.BlockSpec((1, tile, width), lambda bi, p, i: (bi, i, p)),
            pl.BlockSpec((1, seq, width), lambda bi, p, i: (bi, 0, blocks + p)),
            pl.BlockSpec((1, seq, width), lambda bi, p, i: (bi, 0, 2 * blocks + p)),
            pl.BlockSpec(uu.shape, lambda bi, p, i: (0, 0)),
        ],
        out_specs=pl.BlockSpec((1, tile, width), lambda bi, p, i: (bi, i, p)),
        out_shape=jax.ShapeDtypeStruct((b, seq, BRANCH_WIDTH), bf16),
        scratch_shapes=[
            pltpu.VMEM((n_heads, tile, LANES), f32),
            pltpu.VMEM((n_heads, tile, LANES), f32),
        ],
        compiler_params=_params("parallel", "parallel", "arbitrary"),
        name="sb_attn",
    )(qkv, qkv, qkv, uu)


def _lane_fold(x, op):
    return functools.reduce(op, [x[:, t * LANES:(t + 1) * LANES] for t in range(x.shape[1] // LANES)])


def _fox_kernel(qa_ref, qb_ref, k_ref, v_ref, fc_ref, oa_ref, ob_ref, qh_ref, s_ref, m_ref, acc_ref,
                *, tile, n_blocks):
    p = pl.program_id(1)
    t = pl.program_id(2)
    n_slots = n_blocks + 1
    row = lax.broadcasted_iota(jnp.int32, (tile, tile), 0)
    col = lax.broadcasted_iota(jnp.int32, (tile, tile), 1)
    causal = col <= row

    for w, q_ref in enumerate((qa_ref, qb_ref)):
        for h, q_head in enumerate(_head_queries(q_ref[0])):
            qh_ref[w, h] = q_head
    m_ref[...] = jnp.full_like(m_ref, NEG_BIG)
    acc_ref[...] = jnp.zeros_like(acc_ref)

    def owner(c):
        if c == 0:
            return 0, t
        if c == n_slots - 1:
            return 1, n_blocks - 1 - t
        is_b = c > t
        return is_b.astype(jnp.int32), jnp.where(is_b, c - t - 1, t - c)

    def scores(c):
        w, kc = owner(c)
        k = k_ref[0, pl.ds(pl.multiple_of(kc * tile, tile), tile), :]
        for h in range(HEADS_PER_BLOCK):
            head = p * HEADS_PER_BLOCK + h
            f_key = fc_ref[0, pl.ds(head * n_blocks + kc, 1), :]
            s = _dot_nt(qh_ref[w, h], k) * LOG2E - f_key * LOG2E
            if c in (0, n_slots - 1):
                s = jnp.where(causal, s, NEG_BIG)
            m_ref[w, h] = jnp.maximum(m_ref[w, h], _lane_fold(s, jnp.maximum))
            s_ref[h, c] = s

    def finish_max(w):
        for h in range(HEADS_PER_BLOCK):
            m_ref[w, h] = jnp.broadcast_to(jnp.max(m_ref[w, h], axis=1, keepdims=True), (tile, LANES))

    for c in range(1, n_slots - 1):
        scores(c)
    scores(0)
    finish_max(0)
    scores(n_slots - 1)
    finish_max(1)

    ones = jnp.ones((tile, LANES), bf16)
    for c in list(range(1, n_slots - 1)) + [0, n_slots - 1]:
        w, kc = owner(c)
        v = jnp.concatenate([v_ref[0, pl.ds(pl.multiple_of(kc * tile, tile), tile), :], ones], axis=1)
        for h in range(HEADS_PER_BLOCK):
            pr = jnp.exp2(s_ref[h, c] - _lane_tile(m_ref[w, h], tile))
            acc_ref[w, h] += _dot(pr.astype(bf16), v)

    for w, o_ref in enumerate((oa_ref, ob_ref)):
        out = [acc_ref[w, h, :, 0:LANES] / acc_ref[w, h, :, LANES:2 * LANES] for h in range(HEADS_PER_BLOCK)]
        o_ref[0] = _merge_heads(out[0], out[1]).astype(o_ref.dtype)


def _fox_attention(qkv, fc, tile):
    b, seq, _ = qkv.shape
    blocks = BRANCH_WIDTH // LANES
    n_blocks = seq // tile
    half = n_blocks // 2
    base = 3 * blocks
    out = jax.ShapeDtypeStruct((b, seq // 2, BRANCH_WIDTH), bf16)
    return pl.pallas_call(
        functools.partial(_fox_kernel, tile=tile, n_blocks=n_blocks),
        grid=(b, blocks, half),
        in_specs=[
            pl.BlockSpec((1, tile, LANES), lambda bi, p, t: (bi, t, base + p)),
            pl.BlockSpec((1, tile, LANES), lambda bi, p, t: (bi, n_blocks - 1 - t, base + p)),
            pl.BlockSpec((1, seq, LANES), lambda bi, p, t: (bi, 0, base + blocks + p)),
            pl.BlockSpec((1, seq, LANES), lambda bi, p, t: (bi, 0, base + 2 * blocks + p)),
            pl.BlockSpec((1, N_HEADS * n_blocks, tile), lambda bi, p, t: (bi, 0, 0)),
        ],
        out_specs=[
            pl.BlockSpec((1, tile, LANES), lambda bi, p, t: (bi, t, p)),
            pl.BlockSpec((1, tile, LANES), lambda bi, p, t: (bi, half - 1 - t, p)),
        ],
        out_shape=[out, out],
        scratch_shapes=[
            pltpu.VMEM((2, HEADS_PER_BLOCK, tile, LANES), bf16),
            pltpu.VMEM((HEADS_PER_BLOCK, n_blocks + 1, tile, tile), f32),
            pltpu.VMEM((2, HEADS_PER_BLOCK, tile, LANES), f32),
            pltpu.VMEM((2, HEADS_PER_BLOCK, tile, 2 * LANES), f32),
        ],
        compiler_params=_params("parallel", "parallel", "arbitrary"),
        name="fox_attn",
    )(qkv, qkv, qkv, qkv, fc)


def _post_kernel(x_ref, sh_ref, sc_ref, g1_ref, asb_ref, afx_lo_ref, afx_hi_ref, wg_ref, bg_ref, wsb_ref,
                 wfx_ref, wo_ref, lg_ref, lb_ref, o_ref, *, alpha, tiles_per_seq):
    x = x_ref[...]
    d = x.shape[1]
    u = (_layer_norm(x) * (1.0 + sc_ref[0]) + sh_ref[0]).astype(bf16)
    y_sb = _dot(asb_ref[...], wsb_ref[...])
    in_first_half = (pl.program_id(0) % tiles_per_seq) < tiles_per_seq // 2
    a_fx = jnp.where(in_first_half, afx_lo_ref[...], afx_hi_ref[...])
    y_fx = _dot(a_fx, wfx_ref[...])
    g_sb = _sigmoid(_dot_nt(u, wg_ref[0:d, :]) + bg_ref[:, 0:d])
    g_fx = _sigmoid(_dot_nt(u, wg_ref[d:2 * d, :]) + bg_ref[:, d:2 * d])
    mixed = (g_sb * y_sb + g_fx * y_fx).astype(bf16)
    mix = _dot(mixed, wo_ref[...])
    r = alpha * x + g1_ref[0] * mix
    o_ref[...] = _layer_norm(r) * lg_ref[...] + lb_ref[...]


def _post(x2, sh, sc, g1, asb, afx_lo, afx_hi, wg, bg, wsb, wfx, wo, lg, lb, seq, alpha):
    n_tok, d = x2.shape
    tile = ROW_TILE
    tiles_per_seq = seq // tile
    half = tiles_per_seq // 2
    bmap = lambda r: (r // tiles_per_seq, 0, 0)
    lo_map = lambda r: ((r // tiles_per_seq) * half + jnp.minimum(r % tiles_per_seq, half - 1), 0)
    hi_map = lambda r: ((r // tiles_per_seq) * half + jnp.maximum(r % tiles_per_seq - half, 0), 0)
    full = lambda a: pl.BlockSpec(a.shape, lambda r: (0,) * a.ndim)
    return pl.pallas_call(
        functools.partial(_post_kernel, alpha=alpha, tiles_per_seq=tiles_per_seq),
        grid=(n_tok // tile,),
        in_specs=[
            pl.BlockSpec((tile, d), lambda r: (r, 0)),
            pl.BlockSpec((1, 1, d), bmap),
            pl.BlockSpec((1, 1, d), bmap),
            pl.BlockSpec((1, 1, d), bmap),
            pl.BlockSpec((tile, BRANCH_WIDTH), lambda r: (r, 0)),
            pl.BlockSpec((tile, BRANCH_WIDTH), lo_map),
            pl.BlockSpec((tile, BRANCH_WIDTH), hi_map),
            full(wg), full(bg), full(wsb), full(wfx), full(wo), full(lg), full(lb),
        ],
        out_specs=pl.BlockSpec((tile, d), lambda r: (r, 0)),
        out_shape=jax.ShapeDtypeStruct((n_tok, d), f32),
        compiler_params=_params("arbitrary"),
        name="post",
    )(x2, sh, sc, g1, asb, afx_lo, afx_hi, wg, bg, wsb, wfx, wo, lg, lb)


def _ffn_kernel(x_ref, sh_ref, sc_ref, g2_ref, wg_ref, wu_ref, wd_ref, lg_ref, lb_ref, o_ref, acc_ref,
                *, alpha):
    x = x_ref[...]
    u = (_layer_norm(x) * (1.0 + sc_ref[0]) + sh_ref[0]).astype(bf16)
    d_ff = wg_ref.shape[1]
    for c0 in range(0, d_ff, FFN_CHUNK):
        gate = _dot(u, wg_ref[:, c0:c0 + FFN_CHUNK].astype(bf16))
        up = _dot(u, wu_ref[:, c0:c0 + FFN_CHUNK].astype(bf16))
        h = (gate * _sigmoid(gate) * up).astype(bf16)
        part = _dot(h, wd_ref[c0:c0 + FFN_CHUNK, :].astype(bf16))
        if c0 == 0:
            acc_ref[...] = part
        else:
            acc_ref[...] += part
    r = alpha * x + g2_ref[0] * acc_ref[...]
    o_ref[...] = _layer_norm(r) * lg_ref[...] + lb_ref[...]


def _ffn(x2, sh, sc, g2, wg, wu, wd, lg, lb, seq, alpha):
    n_tok, d = x2.shape
    tile = ROW_TILE
    tiles_per_seq = seq // tile
    bmap = lambda r: (r // tiles_per_seq, 0, 0)
    resident = lambda a: pl.BlockSpec(a.shape, lambda r: (0,) * a.ndim, pipeline_mode=pl.Buffered(1))
    return pl.pallas_call(
        functools.partial(_ffn_kernel, alpha=alpha),
        grid=(n_tok // tile,),
        in_specs=[
            pl.BlockSpec((tile, d), lambda r: (r, 0)),
            pl.BlockSpec((1, 1, d), bmap),
            pl.BlockSpec((1, 1, d), bmap),
            pl.BlockSpec((1, 1, d), bmap),
            resident(wg), resident(wu), resident(wd), resident(lg), resident(lb),
        ],
        out_specs=pl.BlockSpec((tile, d), lambda r: (r, 0)),
        out_shape=jax.ShapeDtypeStruct((n_tok, d), f32),
        scratch_shapes=[pltpu.VMEM((tile, d), f32)],
        compiler_params=_params("arbitrary"),
        name="ffn",
    )(x2, sh, sc, g2, wg, wu, wd, lg, lb)


def _suffix_sum_matrix(tile):
    r = jnp.arange(tile)[:, None]
    s = jnp.arange(tile)[None, :]
    u = (r >= s).astype(bf16)
    return jnp.concatenate([u, u], axis=0)


def _inclusive_upper(chunk):
    r = jnp.arange(chunk)[:, None]
    s = jnp.arange(chunk)[None, :]
    return (r <= s).astype(bf16)


def kernel(x, c, w_ada, b_ada, w_in, b_gate, b_forget, w_sb_out, w_fox_out, w_o, ln1_g, ln1_b,
           w_ffn_gate, w_ffn_up, w_ffn_down, ln2_g, ln2_b):
    batch, seq, d = x.shape
    depth = w_ada.shape[0]
    alpha = (2 * depth) ** 0.25
    n_tok = batch * seq
    qkv_cols = 6 * BRANCH_WIDTH
    off_fgate = qkv_cols
    off_bgate = off_fgate + N_HEADS
    assert w_in.shape[2] == off_bgate + 2 * d
    assert seq % ATTN_TILE == 0 and seq % (2 * FOX_TILE) == 0 and seq % ROW_TILE == 0
    assert seq % FCUM_CHUNK == 0 and batch <= SUBLANES

    c_pad = jnp.zeros((SUBLANES, d), f32).at[:batch].set(c)
    uu = _suffix_sum_matrix(ATTN_TILE)
    tri = _inclusive_upper(FCUM_CHUNK)

    x2 = x.reshape(n_tok, d)
    for l in range(depth):
        ada = _ada(c_pad, w_ada[l], b_ada[l][None, :])[:batch]
        sh1, sc1, g1, sh2, sc2, g2 = [t[:, None, :] for t in jnp.split(ada, 6, axis=-1)]

        w_in_t = jnp.swapaxes(w_in[l], 0, 1)
        bf_pad = jnp.zeros((1, LANES), f32).at[0, :N_HEADS].set(b_forget[l])

        qkv, flog = _inproj(x2, sh1, sc1, w_in_t, seq)
        qkv = qkv.reshape(batch, seq, qkv_cols)
        fc = _fcum(flog.reshape(batch, seq, LANES), bf_pad, tri)
        fc = fc.reshape(batch, N_HEADS * (seq // FOX_TILE), FOX_TILE)

        a_sb = _sb_attention(qkv, uu, ATTN_TILE, SB_LANE_BLOCKS).reshape(n_tok, BRANCH_WIDTH)
        a_fx_lo, a_fx_hi = [a.reshape(n_tok // 2, BRANCH_WIDTH) for a in _fox_attention(qkv, fc, FOX_TILE)]

        x2 = _post(x2, sh1, sc1, g1, a_sb, a_fx_lo, a_fx_hi,
                   w_in_t[off_bgate:].astype(bf16), b_gate[l][None, :],
                   w_sb_out[l].astype(bf16), w_fox_out[l].astype(bf16), w_o[l].astype(bf16),
                   ln1_g[l][None, :], ln1_b[l][None, :], seq, alpha)
        x2 = _ffn(x2, sh2, sc2, g2,
                  w_ffn_gate[l], w_ffn_up[l], w_ffn_down[l],
                  ln2_g[l][None, :], ln2_b[l][None, :], seq, alpha)
    return x2.reshape(batch, seq, d)
```

```python
import functools

import jax
import jax.numpy as jnp
from jax import lax
from jax.experimental import pallas as pl
from jax.experimental.pallas import tpu as pltpu

HEAD_DIM = 64
N_HEADS = 8
BRANCH_WIDTH = N_HEADS * HEAD_DIM
LANES = 128
SUBLANES = 8
HEADS_PER_BLOCK = LANES // HEAD_DIM
LN_EPS = 1e-5
NEG_BIG = -1e30
LOG2E = 1.4426950408889634
VMEM_LIMIT = 56 * 1024 * 1024
SB_UNDERFLOW_LOG = -105.0

ATTN_TILE = 256
SB_LANE_BLOCKS = 4
FOX_TILE = 512
ROW_TILE = 512
FFN_CHUNK = 256
ADA_COLS = 1024
FCUM_CHUNK = 256

f32 = jnp.float32
bf16 = jnp.bfloat16


def _dot(a, b):
    return jnp.dot(a, b, preferred_element_type=f32)


def _dot_nt(a, b):
    return lax.dot_general(a, b, (((1,), (1,)), ((), ())), preferred_element_type=f32)


def _layer_norm(x):
    mu = jnp.mean(x, axis=-1, keepdims=True)
    xc = x - mu
    var = jnp.mean(xc * xc, axis=-1, keepdims=True)
    return xc * lax.rsqrt(var + LN_EPS)


def _sigmoid(x):
    return 0.5 * jnp.tanh(0.5 * x) + 0.5


def _params(*sem):
    return pltpu.CompilerParams(dimension_semantics=sem, vmem_limit_bytes=VMEM_LIMIT)


def _ada_kernel(c_ref, w_ref, b_ref, o_ref):
    c = c_ref[...]
    c_act = (c * _sigmoid(c)).astype(bf16)
    o_ref[...] = _dot(c_act, w_ref[...].astype(bf16)) + b_ref[...]


def _ada(c_pad, w, b):
    rows, d = c_pad.shape
    n = w.shape[1]
    return pl.pallas_call(
        _ada_kernel,
        grid=(n // ADA_COLS,),
        in_specs=[
            pl.BlockSpec((rows, d), lambda j: (0, 0)),
            pl.BlockSpec((d, ADA_COLS), lambda j: (0, j)),
            pl.BlockSpec((1, ADA_COLS), lambda j: (0, j)),
        ],
        out_specs=pl.BlockSpec((rows, ADA_COLS), lambda j: (0, j)),
        out_shape=jax.ShapeDtypeStruct((rows, n), f32),
        compiler_params=_params("arbitrary"),
        name="ada",
    )(c_pad, w, b)


def _inproj_kernel(x_ref, sh_ref, sc_ref, wqkv_ref, wf_ref, qkv_ref, fl_ref):
    u = (_layer_norm(x_ref[...]) * (1.0 + sc_ref[0]) + sh_ref[0]).astype(bf16)
    for g in range(wqkv_ref.shape[0] // BRANCH_WIDTH):
        cols = slice(g * BRANCH_WIDTH, (g + 1) * BRANCH_WIDTH)
        y = _dot_nt(u, wqkv_ref[cols, :].astype(bf16))
        if g % 3 == 0:
            y = y * HEAD_DIM ** -0.5
        qkv_ref[:, cols] = y.astype(bf16)
    fl_ref[...] = _dot_nt(u, wf_ref[...].astype(bf16))


def _inproj(x2, sh, sc, w_in_t, seq):
    n_tok, d = x2.shape
    qkv_cols = 6 * BRANCH_WIDTH
    tiles_per_seq = seq // ROW_TILE
    bmap = lambda r: (r // tiles_per_seq, 0, 0)
    return pl.pallas_call(
        _inproj_kernel,
        grid=(n_tok // ROW_TILE,),
        in_specs=[
            pl.BlockSpec((ROW_TILE, d), lambda r: (r, 0)),
            pl.BlockSpec((1, 1, d), bmap),
            pl.BlockSpec((1, 1, d), bmap),
            pl.BlockSpec((qkv_cols, d), lambda r: (0, 0), pipeline_mode=pl.Buffered(1)),
            pl.BlockSpec((LANES, d), lambda r: (qkv_cols // LANES, 0), pipeline_mode=pl.Buffered(1)),
        ],
        out_specs=[
            pl.BlockSpec((ROW_TILE, qkv_cols), lambda r: (r, 0)),
            pl.BlockSpec((ROW_TILE, LANES), lambda r: (r, 0)),
        ],
        out_shape=[
            jax.ShapeDtypeStruct((n_tok, qkv_cols), bf16),
            jax.ShapeDtypeStruct((n_tok, LANES), f32),
        ],
        compiler_params=_params("arbitrary"),
        name="inproj",
    )(x2, sh, sc, w_in_t, w_in_t)


def _split3(x):
    hi = x.astype(bf16)
    r1 = x - hi.astype(f32)
    mid = r1.astype(bf16)
    lo = (r1 - mid.astype(f32)).astype(bf16)
    return hi, mid, lo


def _fcum_kernel(fl_ref, bf_ref, tri_ref, o_ref, *, chunk):
    seq = fl_ref.shape[1]
    tri = tri_ref[...]
    carry = jnp.zeros((N_HEADS, 1), f32)
    for c0 in range(0, seq, chunk):
        logit = fl_ref[0, c0:c0 + chunk, :] + bf_ref[...]
        ls = jnp.minimum(logit, 0.0) - jnp.log1p(jnp.exp(-jnp.abs(logit)))
        ls_t = ls.T[0:N_HEADS, :]
        hi, mid, lo = _split3(ls_t)
        cs = _dot(hi, tri) + _dot(mid, tri) + _dot(lo, tri) + carry
        o_ref[0, :, c0:c0 + chunk] = cs
        carry = cs[:, chunk - 1:chunk]


def _fcum(fl3, bf_pad, tri):
    b, seq, _ = fl3.shape
    chunk = tri.shape[0]
    return pl.pallas_call(
        functools.partial(_fcum_kernel, chunk=chunk),
        grid=(b,),
        in_specs=[
            pl.BlockSpec((1, seq, LANES), lambda i: (i, 0, 0)),
            pl.BlockSpec((1, LANES), lambda i: (0, 0)),
            pl.BlockSpec(tri.shape, lambda i: (0, 0)),
        ],
        out_specs=pl.BlockSpec((1, N_HEADS, seq), lambda i: (i, 0, 0)),
        out_shape=jax.ShapeDtypeStruct((b, N_HEADS, seq), f32),
        compiler_params=_params("arbitrary"),
        name="fcum",
    )(fl3, bf_pad, tri)


def _head_queries(q):
    lane = lax.broadcasted_iota(jnp.int32, q.shape, 1)
    zero = jnp.zeros_like(q)
    return [jnp.where(lane < HEAD_DIM, q, zero), jnp.where(lane >= HEAD_DIM, q, zero)]


def _lane_tile(x, width):
    reps = width // LANES
    return x if reps == 1 else jnp.concatenate([x] * reps, axis=1)


def _merge_heads(o0, o1):
    lane = lax.broadcasted_iota(jnp.int32, o0.shape, 1)
    return jnp.where(lane < HEAD_DIM, o0, o1)


def _sb_suffix(u, mask, uu):
    log_1m = jnp.minimum(u, 0.0) - jnp.log(1.0 + jnp.exp(-jnp.abs(u)))
    if mask is not None:
        log_1m = jnp.where(mask, log_1m, 0.0)
    hi = log_1m.astype(bf16)
    lo = (log_1m - hi.astype(f32)).astype(bf16)
    return _dot(jnp.concatenate([hi, lo], axis=1), uu)


def _sb_total(incl):
    return jnp.broadcast_to(incl[:, 0:1], (incl.shape[0], LANES))


def _sb_weights(u, incl, carry, mask):
    g = incl if carry is None else incl + _lane_tile(carry, u.shape[1])
    a = jnp.exp(g - u)
    if mask is not None:
        a = jnp.where(mask, a, 0.0)
    return a.astype(bf16)


def _sb_block(u, carry, mask, uu):
    incl = _sb_suffix(u, mask, uu)
    return _sb_weights(u, incl, carry, mask), _sb_total(incl)


def _sb_kernel(q_ref, k_ref, v_ref, uu_ref, o_ref, acc_ref, car_ref, *, tile, lane_blocks):
    i = pl.program_id(2)
    n_heads = lane_blocks * HEADS_PER_BLOCK
    heads = range(n_heads)
    uu = uu_ref[...]
    row = lax.broadcasted_iota(jnp.int32, (tile, tile), 0)
    col = lax.broadcasted_iota(jnp.int32, (tile, tile), 1)
    below = col < row

    def lanes_of(h):
        lb = h // HEADS_PER_BLOCK
        return slice(lb * LANES, (lb + 1) * LANES)

    qh = []
    for lb in range(lane_blocks):
        qh += _head_queries(-q_ref[0, :, lb * LANES:(lb + 1) * LANES])

    @pl.when(i == 0)
    def _():
        for h in heads:
            a, total = _sb_block(_dot_nt(qh[h], k_ref[0, 0:tile, lanes_of(h)]), None, below, uu)
            acc_ref[h] = _dot(a, v_ref[0, 0:tile, lanes_of(h)])
            car_ref[h] = total

    @pl.when(i > 0)
    def _():
        rows = pl.ds(pl.multiple_of((i - 1) * tile, tile), 2 * tile)
        u = [_dot_nt(qh[h], k_ref[0, rows, lanes_of(h)]) for h in heads]
        s_diag = [_sb_suffix(u[h][:, tile:], below, uu) for h in heads]
        s_prev = [_sb_suffix(u[h][:, :tile], None, uu) for h in heads]
        for h in heads:
            t_diag = _sb_total(s_diag[h])
            a_diag = _sb_weights(u[h][:, tile:], s_diag[h], None, below)
            a_prev = _sb_weights(u[h][:, :tile], s_prev[h], t_diag, None)
            acc_ref[h] = _dot(jnp.concatenate([a_prev, a_diag], axis=1), v_ref[0, rows, lanes_of(h)])
            car_ref[h] = t_diag + _sb_total(s_prev[h])

    def exhausted():
        worst = functools.reduce(jnp.maximum, [car_ref[h] for h in heads])
        return jnp.max(worst) < SB_UNDERFLOW_LOG

    def cond(state):
        j, done = state
        return jnp.logical_and(j >= 0, jnp.logical_not(done))

    def body(state):
        j, _ = state
        rows = pl.ds(pl.multiple_of(j * tile, tile), tile)
        for h in heads:
            a, total = _sb_block(_dot_nt(qh[h], k_ref[0, rows, lanes_of(h)]), car_ref[h], None, uu)
            acc_ref[h] += _dot(a, v_ref[0, rows, lanes_of(h)])
            car_ref[h] += total
        return j - 1, exhausted()

    lax.while_loop(cond, body, (i - 2, exhausted()))
    for lb in range(lane_blocks):
        h0 = lb * HEADS_PER_BLOCK
        o_ref[0, :, lb * LANES:(lb + 1) * LANES] = _merge_heads(acc_ref[h0], acc_ref[h0 + 1]).astype(o_ref.dtype)


def _sb_attention(qkv, uu, tile, lane_blocks):
    b, seq, _ = qkv.shape
    width = lane_blocks * LANES
    blocks = BRANCH_WIDTH // width
    n_heads = lane_blocks * HEADS_PER_BLOCK
    return pl.pallas_call(
        functools.partial(_sb_kernel, tile=tile, lane_blocks=lane_blocks),
        grid=(b, blocks, seq // tile),
        in_specs=[
            pl.BlockSpec((1, tile, width), lambda bi, p, i: (bi, i, p)),
            pl.BlockSpec((1, seq, width), lambda bi, p, i: (bi, 0, blocks + p)),
            pl.BlockSpec((1, seq, width), lambda bi, p, i: (bi, 0, 2 * blocks + p)),
            pl.BlockSpec(uu.shape, lambda bi, p, i: (0, 0)),
        ],
        out_specs=pl.BlockSpec((1, tile, width), lambda bi, p, i: (bi, i, p)),
        out_shape=jax.ShapeDtypeStruct((b, seq, BRANCH_WIDTH), bf16),
        scratch_shapes=[
            pltpu.VMEM((n_heads, tile, LANES), f32),
            pltpu.VMEM((n_heads, tile, LANES), f32),
        ],
        compiler_params=_params("parallel", "parallel", "arbitrary"),
        name="sb_attn",
    )(qkv, qkv, qkv, uu)


def _lane_fold(x, op):
    return functools.reduce(op, [x[:, t * LANES:(t + 1) * LANES] for t in range(x.shape[1] // LANES)])


def _fox_kernel(qa_ref, qb_ref, k_ref, v_ref, fc_ref, oa_ref, ob_ref, qh_ref, s_ref, m_ref, acc_ref,
                *, tile, n_blocks):
    p = pl.program_id(1)
    t = pl.program_id(2)
    n_slots = n_blocks + 1
    row = lax.broadcasted_iota(jnp.int32, (tile, tile), 0)
    col = lax.broadcasted_iota(jnp.int32, (tile, tile), 1)
    causal = col <= row

    for w, q_ref in enumerate((qa_ref, qb_ref)):
        for h, q_head in enumerate(_head_queries(q_ref[0])):
            qh_ref[w, h] = q_head
    m_ref[...] = jnp.full_like(m_ref, NEG_BIG)
    acc_ref[...] = jnp.zeros_like(acc_ref)

    def owner(c):
        if c == 0:
            return 0, t
        if c == n_slots - 1:
            return 1, n_blocks - 1 - t
        is_b = c > t
        return is_b.astype(jnp.int32), jnp.where(is_b, c - t - 1, t - c)

    def scores(c):
        w, kc = owner(c)
        k = k_ref[0, pl.ds(pl.multiple_of(kc * tile, tile), tile), :]
        for h in range(HEADS_PER_BLOCK):
            head = p * HEADS_PER_BLOCK + h
            f_key = fc_ref[0, pl.ds(head * n_blocks + kc, 1), :]
            s = _dot_nt(qh_ref[w, h], k) * LOG2E - f_key * LOG2E
            if c in (0, n_slots - 1):
                s = jnp.where(causal, s, NEG_BIG)
            m_ref[w, h] = jnp.maximum(m_ref[w, h], _lane_fold(s, jnp.maximum))
            s_ref[h, c] = s

    def finish_max(w):
        for h in range(HEADS_PER_BLOCK):
            m_ref[w, h] = jnp.broadcast_to(jnp.max(m_ref[w, h], axis=1, keepdims=True), (tile, LANES))

    for c in range(1, n_slots - 1):
        scores(c)
    scores(0)
    finish_max(0)
    scores(n_slots - 1)
    finish_max(1)

    ones = jnp.ones((tile, LANES), bf16)
    for c in list(range(1, n_slots - 1)) + [0, n_slots - 1]:
        w, kc = owner(c)
        v = jnp.concatenate([v_ref[0, pl.ds(pl.multiple_of(kc * tile, tile), tile), :], ones], axis=1)
        for h in range(HEADS_PER_BLOCK):
            pr = jnp.exp2(s_ref[h, c] - _lane_tile(m_ref[w, h], tile))
            acc_ref[w, h] += _dot(pr.astype(bf16), v)

    for w, o_ref in enumerate((oa_ref, ob_ref)):
        out = [acc_ref[w, h, :, 0:LANES] / acc_ref[w, h, :, LANES:2 * LANES] for h in range(HEADS_PER_BLOCK)]
        o_ref[0] = _merge_heads(out[0], out[1]).astype(o_ref.dtype)


def _fox_attention(qkv, fc, tile):
    b, seq, _ = qkv.shape
    blocks = BRANCH_WIDTH // LANES
    n_blocks = seq // tile
    half = n_blocks // 2
    base = 3 * blocks
    out = jax.ShapeDtypeStruct((b, seq // 2, BRANCH_WIDTH), bf16)
    return pl.pallas_call(
        functools.partial(_fox_kernel, tile=tile, n_blocks=n_blocks),
        grid=(b, blocks, half),
        in_specs=[
            pl.BlockSpec((1, tile, LANES), lambda bi, p, t: (bi, t, base + p)),
            pl.BlockSpec((1, tile, LANES), lambda bi, p, t: (bi, n_blocks - 1 - t, base + p)),
            pl.BlockSpec((1, seq, LANES), lambda bi, p, t: (bi, 0, base + blocks + p)),
            pl.BlockSpec((1, seq, LANES), lambda bi, p, t: (bi, 0, base + 2 * blocks + p)),
            pl.BlockSpec((1, N_HEADS * n_blocks, tile), lambda bi, p, t: (bi, 0, 0)),
        ],
        out_specs=[
            pl.BlockSpec((1, tile, LANES), lambda bi, p, t: (bi, t, p)),
            pl.BlockSpec((1, tile, LANES), lambda bi, p, t: (bi, half - 1 - t, p)),
        ],
        out_shape=[out, out],
        scratch_shapes=[
            pltpu.VMEM((2, HEADS_PER_BLOCK, tile, LANES), bf16),
            pltpu.VMEM((HEADS_PER_BLOCK, n_blocks + 1, tile, tile), f32),
            pltpu.VMEM((2, HEADS_PER_BLOCK, tile, LANES), f32),
            pltpu.VMEM((2, HEADS_PER_BLOCK, tile, 2 * LANES), f32),
        ],
        compiler_params=_params("parallel", "parallel", "arbitrary"),
        name="fox_attn",
    )(qkv, qkv, qkv, qkv, fc)


def _post_kernel(x_ref, sh_ref, sc_ref, g1_ref, asb_ref, afx_lo_ref, afx_hi_ref, wg_ref, bg_ref, wsb_ref,
                 wfx_ref, wo_ref, lg_ref, lb_ref, o_ref, *, alpha, tiles_per_seq):
    x = x_ref[...]
    d = x.shape[1]
    u = (_layer_norm(x) * (1.0 + sc_ref[0]) + sh_ref[0]).astype(bf16)
    y_sb = _dot(asb_ref[...], wsb_ref[...])
    in_first_half = (pl.program_id(0) % tiles_per_seq) < tiles_per_seq // 2
    a_fx = jnp.where(in_first_half, afx_lo_ref[...], afx_hi_ref[...])
    y_fx = _dot(a_fx, wfx_ref[...])
    g_sb = _sigmoid(_dot_nt(u, wg_ref[0:d, :]) + bg_ref[:, 0:d])
    g_fx = _sigmoid(_dot_nt(u, wg_ref[d:2 * d, :]) + bg_ref[:, d:2 * d])
    mixed = (g_sb * y_sb + g_fx * y_fx).astype(bf16)
    mix = _dot(mixed, wo_ref[...])
    r = alpha * x + g1_ref[0] * mix
    o_ref[...] = _layer_norm(r) * lg_ref[...] + lb_ref[...]


def _post(x2, sh, sc, g1, asb, afx_lo, afx_hi, wg, bg, wsb, wfx, wo, lg, lb, seq, alpha):
    n_tok, d = x2.shape
    tile = ROW_TILE
    tiles_per_seq = seq // tile
    half = tiles_per_seq // 2
    bmap = lambda r: (r // tiles_per_seq, 0, 0)
    lo_map = lambda r: ((r // tiles_per_seq) * half + jnp.minimum(r % tiles_per_seq, half - 1), 0)
    hi_map = lambda r: ((r // tiles_per_seq) * half + jnp.maximum(r % tiles_per_seq - half, 0), 0)
    full = lambda a: pl.BlockSpec(a.shape, lambda r: (0,) * a.ndim)
    return pl.pallas_call(
        functools.partial(_post_kernel, alpha=alpha, tiles_per_seq=tiles_per_seq),
        grid=(n_tok // tile,),
        in_specs=[
            pl.BlockSpec((tile, d), lambda r: (r, 0)),
            pl.BlockSpec((1, 1, d), bmap),
            pl.BlockSpec((1, 1, d), bmap),
            pl.BlockSpec((1, 1, d), bmap),
            pl.BlockSpec((tile, BRANCH_WIDTH), lambda r: (r, 0)),
            pl.BlockSpec((tile, BRANCH_WIDTH), lo_map),
            pl.BlockSpec((tile, BRANCH_WIDTH), hi_map),
            full(wg), full(bg), full(wsb), full(wfx), full(wo), full(lg), full(lb),
        ],
        out_specs=pl.BlockSpec((tile, d), lambda r: (r, 0)),
        out_shape=jax.ShapeDtypeStruct((n_tok, d), f32),
        compiler_params=_params("arbitrary"),
        name="post",
    )(x2, sh, sc, g1, asb, afx_lo, afx_hi, wg, bg, wsb, wfx, wo, lg, lb)


def _ffn_kernel(x_ref, sh_ref, sc_ref, g2_ref, wg_ref, wu_ref, wd_ref, lg_ref, lb_ref, o_ref, acc_ref,
                *, alpha):
    x = x_ref[...]
    u = (_layer_norm(x) * (1.0 + sc_ref[0]) + sh_ref[0]).astype(bf16)
    d_ff = wg_ref.shape[1]
    for c0 in range(0, d_ff, FFN_CHUNK):
        gate = _dot(u, wg_ref[:, c0:c0 + FFN_CHUNK].astype(bf16))
        up = _dot(u, wu_ref[:, c0:c0 + FFN_CHUNK].astype(bf16))
        h = (gate * _sigmoid(gate) * up).astype(bf16)
        part = _dot(h, wd_ref[c0:c0 + FFN_CHUNK, :].astype(bf16))
        if c0 == 0:
            acc_ref[...] = part
        else:
            acc_ref[...] += part
    r = alpha * x + g2_ref[0] * acc_ref[...]
    o_ref[...] = _layer_norm(r) * lg_ref[...] + lb_ref[...]


def _ffn(x2, sh, sc, g2, wg, wu, wd, lg, lb, seq, alpha):
    n_tok, d = x2.shape
    tile = ROW_TILE
    tiles_per_seq = seq // tile
    bmap = lambda r: (r // tiles_per_seq, 0, 0)
    resident = lambda a: pl.BlockSpec(a.shape, lambda r: (0,) * a.ndim, pipeline_mode=pl.Buffered(1))
    return pl.pallas_call(
        functools.partial(_ffn_kernel, alpha=alpha),
        grid=(n_tok // tile,),
        in_specs=[
            pl.BlockSpec((tile, d), lambda r: (r, 0)),
            pl.BlockSpec((1, 1, d), bmap),
            pl.BlockSpec((1, 1, d), bmap),
            pl.BlockSpec((1, 1, d), bmap),
            resident(wg), resident(wu), resident(wd), resident(lg), resident(lb),
        ],
        out_specs=pl.BlockSpec((tile, d), lambda r: (r, 0)),
        out_shape=jax.ShapeDtypeStruct((n_tok, d), f32),
        scratch_shapes=[pltpu.VMEM((tile, d), f32)],
        compiler_params=_params("arbitrary"),
        name="ffn",
    )(x2, sh, sc, g2, wg, wu, wd, lg, lb)


def _suffix_sum_matrix(tile):
    r = jnp.arange(tile)[:, None]
    s = jnp.arange(tile)[None, :]
    u = (r >= s).astype(bf16)
    return jnp.concatenate([u, u], axis=0)


def _inclusive_upper(chunk):
    r = jnp.arange(chunk)[:, None]
    s = jnp.arange(chunk)[None, :]
    return (r <= s).astype(bf16)


def kernel(x, c, w_ada, b_ada, w_in, b_gate, b_forget, w_sb_out, w_fox_out, w_o, ln1_g, ln1_b,
           w_ffn_gate, w_ffn_up, w_ffn_down, ln2_g, ln2_b):
    batch, seq, d = x.shape
    depth = w_ada.shape[0]
    alpha = (2 * depth) ** 0.25
    n_tok = batch * seq
    qkv_cols = 6 * BRANCH_WIDTH
    off_fgate = qkv_cols
    off_bgate = off_fgate + N_HEADS
    assert w_in.shape[2] == off_bgate + 2 * d
    assert seq % ATTN_TILE == 0 and seq % (2 * FOX_TILE) == 0 and seq % ROW_TILE == 0
    assert seq % FCUM_CHUNK == 0 and batch <= SUBLANES

    c_pad = jnp.zeros((SUBLANES, d), f32).at[:batch].set(c)
    uu = _suffix_sum_matrix(ATTN_TILE)
    tri = _inclusive_upper(FCUM_CHUNK)

    x2 = x.reshape(n_tok, d)
    for l in range(depth):
        ada = _ada(c_pad, w_ada[l], b_ada[l][None, :])[:batch]
        sh1, sc1, g1, sh2, sc2, g2 = [t[:, None, :] for t in jnp.split(ada, 6, axis=-1)]

        w_in_t = jnp.swapaxes(w_in[l], 0, 1)
        bf_pad = jnp.zeros((1, LANES), f32).at[0, :N_HEADS].set(b_forget[l])

        qkv, flog = _inproj(x2, sh1, sc1, w_in_t, seq)
        qkv = qkv.reshape(batch, seq, qkv_cols)
        fc = _fcum(flog.reshape(batch, seq, LANES), bf_pad, tri)
        fc = fc.reshape(batch, N_HEADS * (seq // FOX_TILE), FOX_TILE)

        a_sb = _sb_attention(qkv, uu, ATTN_TILE, SB_LANE_BLOCKS).reshape(n_tok, BRANCH_WIDTH)
        a_fx_lo, a_fx_hi = [a.reshape(n_tok // 2, BRANCH_WIDTH) for a in _fox_attention(qkv, fc, FOX_TILE)]

        x2 = _post(x2, sh1, sc1, g1, a_sb, a_fx_lo, a_fx_hi,
                   w_in_t[off_bgate:].astype(bf16), b_gate[l][None, :],
                   w_sb_out[l].astype(bf16), w_fox_out[l].astype(bf16), w_o[l].astype(bf16),
                   ln1_g[l][None, :], ln1_b[l][None, :], seq, alpha)
        x2 = _ffn(x2, sh2, sc2, g2,
                  w_ffn_gate[l], w_ffn_up[l], w_ffn_down[l],
                  ln2_g[l][None, :], ln2_b[l][None, :], seq, alpha)
    return x2.reshape(batch, seq, d)
```
